```python
import numpy as np
import jax
import jax.numpy as jnp
from jax import lax

D_MODEL = 1024
BATCH = 2
SEQ = 16384
DEPTH = 4

GRID_W = 64
CTX_LEN = 256
N_MIXERS = 2
N_LAYERS_NA = (DEPTH + N_MIXERS - 1) // N_MIXERS
N_LAYERS_LRU = DEPTH // N_MIXERS
N_SUBLAYERS = 3
N_MOD = 3 * N_SUBLAYERS
FFN_HIDDEN = 256 * ((8 * D_MODEL // 3 + 255) // 256)
NA_HEADS = 16
NA_HEAD_DIM = D_MODEL // NA_HEADS
NA_WIN_ROWS = 8
NA_WIN_COLS = 16
ROPE_BASE = 10000.0
ROPE_PAIRS_PER_AXIS = NA_HEAD_DIM // 4
LRU_WIDTH = 5 * D_MODEL // 4
LRU_BLOCKS = 10
LRU_BLOCK_W = LRU_WIDTH // LRU_BLOCKS
LRU_C = 8.0
CONV_W = 4
CONV_PAD_LEFT = 2
CONV_PAD_RIGHT = CONV_W - 1 - CONV_PAD_LEFT
RMS_EPS = 1e-6
FFN_RES_WEIGHT = 0.5

kernel_name = 'hybrid_natten_rglru_macaron_dit'


def rms_norm(x, gain):
    x32 = x.astype(jnp.float32)
    y = x32 * lax.rsqrt(jnp.mean(x32 * x32, axis=-1, keepdims=True) + RMS_EPS)
    return (y * gain.astype(jnp.float32)).astype(x.dtype)


def modulate(u, gain, mod, s):
    return rms_norm(u, gain) * (1.0 + mod[:, 3 * s + 1]) + mod[:, 3 * s]


def gated_residual(u, y, gain, mod, s, weight):
    return u + weight * mod[:, 3 * s + 2] * rms_norm(y, gain)


def swiglu(h, w_gate, w_up, w_down):
    return (jax.nn.silu(h @ w_gate) * (h @ w_up)) @ w_down


def axial_rope_tables(n_tok):
    t = jnp.arange(n_tok, dtype=jnp.int32)
    row = (t // GRID_W).astype(jnp.float32)
    col = (t % GRID_W).astype(jnp.float32)
    inv_freq = jnp.power(ROPE_BASE, -jnp.arange(ROPE_PAIRS_PER_AXIS, dtype=jnp.float32) / ROPE_PAIRS_PER_AXIS)
    ang = jnp.concatenate([row[:, None] * inv_freq, col[:, None] * inv_freq], axis=-1)
    return jnp.cos(ang), jnp.sin(ang)


def apply_rope(x, cos, sin):
    x1, x2 = jnp.split(x.astype(jnp.float32), 2, axis=-1)
    cs, sn = cos[None, :, None, :], sin[None, :, None, :]
    return jnp.concatenate([x1 * cs - x2 * sn, x1 * sn + x2 * cs], axis=-1).astype(x.dtype)


def neighbourhood_attention(h, hc, w_qkv, w_o, rpb, cos, sin, need_ctx):
    bsz, n_tok, _ = h.shape
    rows = n_tok // GRID_W
    wr = min(NA_WIN_ROWS, rows)
    wc = min(NA_WIN_COLS, GRID_W)
    scale = NA_HEAD_DIM ** -0.5

    def heads(u):
        return u.reshape(u.shape[0], u.shape[1], NA_HEADS, NA_HEAD_DIM)

    q, k, v = (heads(t) for t in jnp.split(h @ w_qkv, 3, axis=-1))
    q = apply_rope(q, cos, sin)
    k = apply_rope(k, cos, sin)
    kc, vc = (heads(t) for t in jnp.split(hc @ w_qkv[:, D_MODEL:], 2, axis=-1))

    def grid(t):
        return t.reshape(bsz, rows, GRID_W, NA_HEADS, NA_HEAD_DIM)

    q_g, k_g, v_g = grid(q), grid(k), grid(v)

    col = np.arange(GRID_W)
    col_start = np.clip(col - wc // 2, 0, GRID_W - wc)
    col_idx = col_start[:, None] + np.arange(wc)[None, :]
    dc_idx = col_idx - col[:, None] + NA_WIN_COLS - 1
    rpb_cols = rpb[:, :, dc_idx]

    def row_block(r):
        r0 = jnp.clip(r - wr // 2, 0, rows - wr)
        k_win = lax.dynamic_slice_in_dim(k_g, r0, wr, axis=1)[:, :, col_idx]
        v_win = lax.dynamic_slice_in_dim(v_g, r0, wr, axis=1)[:, :, col_idx]
        dr_idx = r0 + jnp.arange(wr, dtype=jnp.int32) - r + NA_WIN_ROWS - 1
        bias = jnp.take(rpb_cols, dr_idx, axis=1).transpose(0, 2, 1, 3)
        q_r = lax.dynamic_index_in_dim(q_g, r, axis=1, keepdims=False)
        s_lat = jnp.einsum('bqhd,bkqjhd->bhqkj', q_r, k_win) * scale + bias[None]
        s_ctx = jnp.einsum('bqhd,bchd->bhqc', q_r, kc) * scale
        s_all = jnp.concatenate([s_lat.reshape(bsz, NA_HEADS, GRID_W, wr * wc), s_ctx], axis=-1)
        p = jax.nn.softmax(s_all.astype(jnp.float32), axis=-1).astype(v.dtype)
        p_lat = p[..., :wr * wc].reshape(bsz, NA_HEADS, GRID_W, wr, wc)
        p_ctx = p[..., wr * wc:]
        return (jnp.einsum('bhqkj,bkqjhd->bqhd', p_lat, v_win)
                + jnp.einsum('bhqc,bchd->bqhd', p_ctx, vc))

    o = lax.map(row_block, jnp.arange(rows, dtype=jnp.int32))
    y = o.transpose(1, 0, 2, 3, 4).reshape(bsz, n_tok, D_MODEL) @ w_o
    if not need_ctx:
        return y, None
    qc = heads(hc @ w_qkv[:, :D_MODEL])
    s_c = jnp.einsum('bqhd,bkhd->bhqk', qc, kc) * scale
    p_c = jax.nn.softmax(s_c.astype(jnp.float32), axis=-1).astype(vc.dtype)
    yc = jnp.einsum('bhqk,bkhd->bqhd', p_c, vc).reshape(bsz, hc.shape[1], D_MODEL) @ w_o
    return y, yc


def depthwise_conv(u, w, b):
    n = u.shape[1]
    up = jnp.pad(u, ((0, 0), (CONV_PAD_LEFT, CONV_PAD_RIGHT), (0, 0)))
    out = up[:, 0:n] * w[0] + b
    for tap in range(1, CONV_W):
        out = out + up[:, tap:tap + n] * w[tap]
    return out


def lru_gates(u, w_a, b_a, w_x, b_x, lam):
    bsz, n, _ = u.shape
    ub = u.reshape(bsz, n, LRU_BLOCKS, LRU_BLOCK_W)
    r = jax.nn.sigmoid(jnp.einsum('blnk,nkj->blnj', ub, w_a).reshape(bsz, n, LRU_WIDTH) + b_a)
    i = jax.nn.sigmoid(jnp.einsum('blnk,nkj->blnj', ub, w_x).reshape(bsz, n, LRU_WIDTH) + b_x)
    log_a = -LRU_C * r.astype(jnp.float32) * jax.nn.softplus(-lam.astype(jnp.float32))
    a = jnp.exp(log_a)
    b = jnp.sqrt(-jnp.expm1(2.0 * log_a)) * (i * u).astype(jnp.float32)
    return a, b


def linear_scan(a, b, reverse):
    def combine(e1, e2):
        a1, b1 = e1
        a2, b2 = e2
        return a1 * a2, a2 * b1 + b2
    return lax.associative_scan(combine, (a, b), axis=1, reverse=reverse)[1]


def rglru_mixer(h, hc, w_in, conv_w, conv_b, w_a, b_a, w_x, b_x, lam, w_o, need_ctx):
    w_gate, w_rec = w_in[:, :LRU_WIDTH], w_in[:, LRU_WIDTH:]
    u = depthwise_conv(h @ w_rec, conv_w, conv_b)
    uc = depthwise_conv(hc @ w_rec, conv_w, conv_b)
    lat_states, ctx_states = [], []
    for d, reverse in enumerate((False, True)):
        ctx_edge = 0 if reverse else -1
        lat_edge = -1 if reverse else 0
        a_c, b_c = lru_gates(uc, w_a[d], b_a[d], w_x[d], b_x[d], lam[d])
        hs_c = linear_scan(a_c, b_c, reverse)
        a, b = lru_gates(u, w_a[d], b_a[d], w_x[d], b_x[d], lam[d])
        b = b.at[:, lat_edge].add(a[:, lat_edge] * hs_c[:, ctx_edge])
        lat_states.append(linear_scan(a, b, reverse))
        ctx_states.append(hs_c)
    rec = (lat_states[0] + lat_states[1]).astype(h.dtype)
    y = (rec * jax.nn.gelu(h @ w_gate)) @ w_o
    if not need_ctx:
        return y, None
    rec_c = (ctx_states[0] + ctx_states[1]).astype(hc.dtype)
    yc = (rec_c * jax.nn.gelu(hc @ w_gate)) @ w_o
    return y, yc


def setup_inputs(seed: int = 0) -> dict:
    key = jax.random.key(seed)
    ks = jax.random.split(key, 24)
    f32 = jnp.float32

    def nrm(k, shape, scale):
        return jax.random.normal(k, shape, f32) * scale

    a_pow = jax.random.uniform(ks[21], (N_LAYERS_LRU, 2, LRU_WIDTH), f32, 0.9, 0.999)
    s = a_pow ** (1.0 / LRU_C)
    return {
        'x': nrm(ks[0], (BATCH, SEQ, D_MODEL), 1.0),
        'c': nrm(ks[1], (BATCH, D_MODEL), 1.0),
        'ctx': nrm(ks[2], (BATCH, CTX_LEN, D_MODEL), 1.0),
        'c_ctx': nrm(ks[3], (D_MODEL,), 1.0),
        'ada_w': nrm(ks[4], (DEPTH, D_MODEL, N_MOD * D_MODEL), D_MODEL ** -0.5),
        'ada_b': nrm(ks[5], (DEPTH, N_MOD * D_MODEL), 0.01),
        'norm_pre': 1.0 + nrm(ks[6], (DEPTH, N_SUBLAYERS, D_MODEL), 0.05),
        'norm_post': 1.0 + nrm(ks[7], (DEPTH, N_SUBLAYERS, D_MODEL), 0.05),
        'ffn_w_gate': nrm(ks[8], (DEPTH, 2, D_MODEL, FFN_HIDDEN), D_MODEL ** -0.5),
        'ffn_w_up': nrm(ks[9], (DEPTH, 2, D_MODEL, FFN_HIDDEN), D_MODEL ** -0.5),
        'ffn_w_down': nrm(ks[10], (DEPTH, 2, FFN_HIDDEN, D_MODEL), FFN_HIDDEN ** -0.5),
        'na_w_qkv': nrm(ks[11], (N_LAYERS_NA, D_MODEL, 3 * D_MODEL), D_MODEL ** -0.5),
        'na_w_o': nrm(ks[12], (N_LAYERS_NA, D_MODEL, D_MODEL), D_MODEL ** -0.5),
        'na_rpb': nrm(ks[13], (N_LAYERS_NA, NA_HEADS, 2 * NA_WIN_ROWS - 1, 2 * NA_WIN_COLS - 1), 0.1),
        'lru_w_in': nrm(ks[14], (N_LAYERS_LRU, D_MODEL, 2 * LRU_WIDTH), D_MODEL ** -0.5),
        'lru_conv_w': nrm(ks[15], (N_LAYERS_LRU, CONV_W, LRU_WIDTH), CONV_W ** -0.5),
        'lru_conv_b': nrm(ks[16], (N_LAYERS_LRU, LRU_WIDTH), 0.01),
        'lru_w_a': nrm(ks[17], (N_LAYERS_LRU, 2, LRU_BLOCKS, LRU_BLOCK_W, LRU_BLOCK_W), LRU_BLOCK_W ** -0.5),
        'lru_b_a': nrm(ks[18], (N_LAYERS_LRU, 2, LRU_WIDTH), 0.01),
        'lru_w_x': nrm(ks[19], (N_LAYERS_LRU, 2, LRU_BLOCKS, LRU_BLOCK_W, LRU_BLOCK_W), LRU_BLOCK_W ** -0.5),
        'lru_b_x': nrm(ks[20], (N_LAYERS_LRU, 2, LRU_WIDTH), 0.01),
        'lru_lambda': jnp.log(s) - jnp.log1p(-s),
        'lru_w_o': nrm(ks[22], (N_LAYERS_LRU, LRU_WIDTH, D_MODEL), LRU_WIDTH ** -0.5),
    }


def reference(x, c, ctx, c_ctx, ada_w, ada_b, norm_pre, norm_post, ffn_w_gate, ffn_w_up, ffn_w_down,
              na_w_qkv, na_w_o, na_rpb, lru_w_in, lru_conv_w, lru_conv_b, lru_w_a, lru_b_a,
              lru_w_x, lru_b_x, lru_lambda, lru_w_o):
    cos, sin = axial_rope_tables(x.shape[1])
    xc = ctx
    for i in range(DEPTH):
        last = i == DEPTH - 1
        j = i // N_MIXERS
        mod = (jax.nn.silu(c) @ ada_w[i] + ada_b[i]).reshape(x.shape[0], N_MOD, 1, D_MODEL)
        mod_c = (jax.nn.silu(c_ctx) @ ada_w[i] + ada_b[i]).reshape(1, N_MOD, 1, D_MODEL)

        def half_ffn(u, m, s):
            f = s // 2
            y = swiglu(modulate(u, norm_pre[i, s], m, s), ffn_w_gate[i, f], ffn_w_up[i, f], ffn_w_down[i, f])
            return gated_residual(u, y, norm_post[i, s], m, s, FFN_RES_WEIGHT)

        x = half_ffn(x, mod, 0)
        xc = half_ffn(xc, mod_c, 0)
        h = modulate(x, norm_pre[i, 1], mod, 1)
        hc = modulate(xc, norm_pre[i, 1], mod_c, 1)
        if i % N_MIXERS == 0:
            y, yc = neighbourhood_attention(h, hc, na_w_qkv[j], na_w_o[j], na_rpb[j], cos, sin, not last)
        else:
            y, yc = rglru_mixer(h, hc, lru_w_in[j], lru_conv_w[j], lru_conv_b[j], lru_w_a[j], lru_b_a[j],
                                lru_w_x[j], lru_b_x[j], lru_lambda[j], lru_w_o[j], not last)
        x = gated_residual(x, y, norm_post[i, 1], mod, 1, 1.0)
        x = half_ffn(x, mod, 2)
        if not last:
            xc = gated_residual(xc, yc, norm_post[i, 1], mod_c, 1, 1.0)
            xc = half_ffn(xc, mod_c, 2)
    return x
```

```python
import functools

import numpy as np
import jax
import jax.numpy as jnp
from jax import lax
from jax.experimental import pallas as pl
from jax.experimental.pallas import tpu as pltpu

F32 = jnp.float32
BF16 = jnp.bfloat16

GRID_W = 64
NA_HEADS = 16
NA_WIN_ROWS = 8
NA_WIN_COLS = 16
ROPE_BASE = 10000.0
LRU_C = 8.0
CONV_W = 4
CONV_PAD_LEFT = 2
RMS_EPS = 1e-6
FFN_RES_WEIGHT = 0.5

LANES = 128
SUBLANES = 8
MXU_DIM = 256

FFN_ROWS = 512
FFN_HID_CHUNK = MXU_DIM
PROJ_ROWS = 512
ATT_ROWS = 4
SCAN_ROWS = 512
MOD_COLS = 1024
VMEM_LIMIT = 56 * 1024 * 1024
NEG_BIG = -1e30


def _cparams(sem):
    return pltpu.CompilerParams(dimension_semantics=sem, vmem_limit_bytes=VMEM_LIMIT)


def _const_spec(shape):
    nd = len(shape)
    return pl.BlockSpec(shape, lambda *_: (0,) * nd, pipeline_mode=pl.Buffered(1))


def _rms(x, gain):
    ms = jnp.mean(x * x, axis=-1, keepdims=True)
    return x * lax.rsqrt(ms + RMS_EPS) * gain


def _modulate(x, npre, mod, s):
    return _rms(x, npre[s:s + 1]) * (1.0 + mod[3 * s + 1:3 * s + 2]) + mod[3 * s:3 * s + 1]


def _mod_kernel(c_ref, w_ref, b_ref, o_ref):
    c = c_ref[...]
    s = (c * jax.nn.sigmoid(c)).astype(BF16)
    o_ref[...] = jnp.dot(s, w_ref[...].astype(BF16), preferred_element_type=F32) + b_ref[...]


def _adaln(cond, ada_w, ada_b):
    depth, d, n = ada_w.shape
    rows = cond.shape[0]
    return pl.pallas_call(
        _mod_kernel,
        grid=(depth, n // MOD_COLS),
        in_specs=[
            pl.BlockSpec((rows, d), lambda l, j: (0, 0)),
            pl.BlockSpec((None, d, MOD_COLS), lambda l, j: (l, 0, j)),
            pl.BlockSpec((None, 1, MOD_COLS), lambda l, j: (l, 0, j)),
        ],
        out_specs=pl.BlockSpec((None, rows, MOD_COLS), lambda l, j: (l, 0, j)),
        out_shape=jax.ShapeDtypeStruct((depth, rows, n), F32),
        compiler_params=_cparams(("arbitrary", "arbitrary")),
        name="adaln_mod",
    )(cond, ada_w, ada_b.reshape(depth, 1, n))


def _ffn_kernel(*refs, s, has_mix):
    if has_mix:
        x_ref, a_ref, wo_ref, mod_ref, npre_ref, npost_ref, wg_ref, wu_ref, wd_ref, o_ref = refs
    else:
        x_ref, mod_ref, npre_ref, npost_ref, wg_ref, wu_ref, wd_ref, o_ref = refs
    x = x_ref[...]
    mod = mod_ref[...]
    npre = npre_ref[...]
    npost = npost_ref[...]
    if has_mix:
        y = jnp.dot(a_ref[...], wo_ref[...], preferred_element_type=F32)
        x = x + mod[5:6] * _rms(y, npost[1:2])
    h = _modulate(x, npre, mod, s).astype(BF16)
    hid = wg_ref.shape[1]
    acc = None
    for c in range(0, hid, FFN_HID_CHUNK):
        g = jnp.dot(h, wg_ref[:, c:c + FFN_HID_CHUNK], preferred_element_type=F32)
        u = jnp.dot(h, wu_ref[:, c:c + FFN_HID_CHUNK], preferred_element_type=F32)
        a = (g * jax.nn.sigmoid(g) * u).astype(BF16)
        part = jnp.dot(a, wd_ref[c:c + FFN_HID_CHUNK, :], preferred_element_type=F32)
        acc = part if acc is None else acc + part
    o_ref[...] = x + FFN_RES_WEIGHT * mod[3 * s + 2:3 * s + 3] * _rms(acc, npost[s:s + 1])


def _ffn(x2d, rows_per_batch, mod, npre, npost, wg, wu, wd, s, mix=None):
    n, d = x2d.shape
    tm = min(FFN_ROWS, n)
    if mod.shape[0] > 1:
        mod_map = lambda i: ((i * tm) // rows_per_batch, 0, 0)
    else:
        mod_map = lambda i: (0, 0, 0)
    row_spec = pl.BlockSpec((tm, d), lambda i: (i, 0))
    in_specs = [row_spec]
    args = [x2d]
    if mix is not None:
        a2d, wo = mix
        in_specs += [pl.BlockSpec((tm, a2d.shape[1]), lambda i: (i, 0)), _const_spec(wo.shape)]
        args += [a2d, wo]
    in_specs += [pl.BlockSpec((None,) + mod.shape[1:], mod_map), _const_spec(npre.shape),
                 _const_spec(npost.shape), _const_spec(wg.shape), _const_spec(wu.shape),
                 _const_spec(wd.shape)]
    args += [mod, npre, npost, wg, wu, wd]
    return pl.pallas_call(
        functools.partial(_ffn_kernel, s=s, has_mix=mix is not None),
        grid=(n // tm,),
        in_specs=in_specs,
        out_specs=row_spec,
        out_shape=jax.ShapeDtypeStruct((n, d), F32),
        compiler_params=_cparams(("arbitrary",)),
        name="mix_ffn" if mix is not None else "ffn",
    )(*args)


def _qkv_kernel(x_ref, mod_ref, npre_ref, cos_ref, sin_ref, w_ref, q_ref, k_ref, v_ref, *, scale):
    d = x_ref.shape[1]
    half = d // NA_HEADS // 2
    h = _modulate(x_ref[...], npre_ref[...], mod_ref[...], 1).astype(BF16)
    reps = d // cos_ref.shape[1]
    cosf = jnp.tile(cos_ref[...], (1, reps))
    sinf = jnp.tile(sin_ref[...], (1, reps))
    lane = lax.broadcasted_iota(jnp.int32, (x_ref.shape[0], d), 1)
    first = (lane & (2 * half - 1)) < half

    def rope(t):
        partner = jnp.where(first, pltpu.roll(t, d - half, 1), pltpu.roll(t, half, 1))
        return t * cosf + partner * sinf

    q = jnp.dot(h, w_ref[:, 0:d], preferred_element_type=F32)
    q_ref[...] = (rope(q) * scale).astype(BF16)
    k = jnp.dot(h, w_ref[:, d:2 * d], preferred_element_type=F32)
    k_ref[...] = rope(k).astype(BF16)
    v = jnp.dot(h, w_ref[:, 2 * d:3 * d], preferred_element_type=F32)
    v_ref[...] = v.astype(BF16)


def _qkv(x2d, rows_per_batch, mod, npre, cos_t, sin_t, w_qkv):
    n, d = x2d.shape
    tm = min(PROJ_ROWS, rows_per_batch)
    tiles_per_batch = rows_per_batch // tm
    if mod.shape[0] > 1:
        mod_map = lambda i: (i // tiles_per_batch, 0, 0)
    else:
        mod_map = lambda i: (0, 0, 0)
    row_spec = pl.BlockSpec((tm, d), lambda i: (i, 0))
    tab_spec = pl.BlockSpec((tm, cos_t.shape[1]), lambda i: (i % tiles_per_batch, 0))
    out = jax.ShapeDtypeStruct((n, d), BF16)
    return pl.pallas_call(
        functools.partial(_qkv_kernel, scale=float((d // NA_HEADS) ** -0.5)),
        grid=(n // tm,),
        in_specs=[row_spec, pl.BlockSpec((None,) + mod.shape[1:], mod_map), _const_spec(npre.shape),
                  tab_spec, tab_spec, _const_spec(w_qkv.shape)],
        out_specs=[row_spec, row_spec, row_spec],
        out_shape=[out, out, out],
        compiler_params=_cparams(("arbitrary",)),
        name="qkv_rope",
    )(x2d, mod, npre, cos_t, sin_t, w_qkv)


def _softmax_pv(qh, kw, vw, kc, vc, bias):
    nt = (((1,), (1,)), ((), ()))
    s_lat = lax.dot_general(qh, kw, nt, preferred_element_type=F32)
    if bias is not None:
        s_lat = s_lat + bias
    s_ctx = lax.dot_general(qh, kc, nt, preferred_element_type=F32)
    m = jnp.maximum(jnp.max(s_lat, axis=-1, keepdims=True), jnp.max(s_ctx, axis=-1, keepdims=True))
    p_lat = jnp.exp(s_lat - m)
    p_ctx = jnp.exp(s_ctx - m)
    denom = jnp.sum(p_lat, axis=-1, keepdims=True) + jnp.sum(p_ctx, axis=-1, keepdims=True)
    o = (jnp.dot(p_lat.astype(BF16), vw, preferred_element_type=F32)
         + jnp.dot(p_ctx.astype(BF16), vc, preferred_element_type=F32))
    return o / denom


def _attn_kernel(q_ref, k_ref, v_ref, kc_ref, vc_ref, bias_ref, o_ref, *, n_rows):
    rq = q_ref.shape[0]
    nk = bias_ref.shape[-1]
    head_dim = q_ref.shape[1] // 2
    rb = pl.program_id(2)
    w0 = jnp.clip(rb * ATT_ROWS - NA_WIN_ROWS // 2, 0, n_rows - nk // GRID_W)
    start = pl.multiple_of(w0 * GRID_W, GRID_W)
    kw = k_ref[pl.ds(start, nk), :]
    vw = v_ref[pl.ds(start, nk), :]
    kc = kc_ref[...]
    vc = vc_ref[...]
    q = q_ref[...]
    lane = lax.broadcasted_iota(jnp.int32, q.shape, 1)
    in_head0 = lane < head_dim
    o0 = _softmax_pv(jnp.where(in_head0, q, jnp.zeros_like(q)), kw, vw, kc, vc, bias_ref[0])
    o1 = _softmax_pv(jnp.where(in_head0, jnp.zeros_like(q), q), kw, vw, kc, vc, bias_ref[1])
    o_ref[...] = jnp.where(in_head0, o0, o1).astype(o_ref.dtype)


def _attn_bias_tables(rpb, n_rows):
    heads = rpb.shape[0]
    win_rows = min(NA_WIN_ROWS, n_rows)
    win_cols = min(NA_WIN_COLS, GRID_W)
    key_rows = ATT_ROWS + NA_WIN_ROWS - 1
    n_blocks = n_rows // ATT_ROWS
    tabs = []
    for blk in (0, 1, n_blocks - 1):
        r_b = blk * ATT_ROWS
        w0 = int(np.clip(r_b - NA_WIN_ROWS // 2, 0, n_rows - key_rows))
        r = r_b + np.arange(ATT_ROWS)
        kr = w0 + np.arange(key_rows)
        r0 = np.clip(r - win_rows // 2, 0, n_rows - win_rows)
        valid_r = (kr[None, :] >= r0[:, None]) & (kr[None, :] < r0[:, None] + win_rows)
        dr = kr[None, :] - r[:, None] + NA_WIN_ROWS - 1
        col = np.arange(GRID_W)
        c0 = np.clip(col - win_cols // 2, 0, GRID_W - win_cols)
        valid_c = (col[None, :] >= c0[:, None]) & (col[None, :] < c0[:, None] + win_cols)
        dc = col[None, :] - col[:, None] + NA_WIN_COLS - 1
        valid = valid_r[:, None, :, None] & valid_c[None, :, None, :]
        dr_i = np.clip(np.broadcast_to(dr[:, None, :, None], valid.shape), 0, rpb.shape[1] - 1)
        dc_i = np.clip(np.broadcast_to(dc[None, :, None, :], valid.shape), 0, rpb.shape[2] - 1)
        rq = ATT_ROWS * GRID_W
        nk = key_rows * GRID_W
        g = rpb[:, dr_i.reshape(rq, nk), dc_i.reshape(rq, nk)]
        tabs.append(jnp.where(jnp.asarray(valid.reshape(rq, nk))[None], g, NEG_BIG))
    t = jnp.stack(tabs, axis=0)
    return t.reshape(3, heads // 2, 2, t.shape[-2], t.shape[-1]).astype(F32)


def _attention(q, k, v, kc, vc, bias_tabs):
    bsz, n_tok, d = q.shape
    n_ctx = kc.shape[1]
    n_rows = n_tok // GRID_W
    n_blocks = n_rows // ATT_ROWS
    rq = ATT_ROWS * GRID_W
    pair = 2 * (d // NA_HEADS)
    q_spec = pl.BlockSpec((None, rq, pair), lambda b, hp, rb: (b, rb, hp))
    seq_spec = pl.BlockSpec((None, n_tok, pair), lambda b, hp, rb: (b, 0, hp))
    ctx_spec = pl.BlockSpec((None, n_ctx, pair), lambda b, hp, rb: (b, 0, hp))

    def bias_map(b, hp, rb):
        kind = jnp.where(rb == 0, 0, jnp.where(rb == n_blocks - 1, 2, 1))
        return (kind, hp, 0, 0, 0)

    bias_spec = pl.BlockSpec((None, None) + bias_tabs.shape[2:], bias_map)
    return pl.pallas_call(
        functools.partial(_attn_kernel, n_rows=n_rows),
        grid=(bsz, d // pair, n_blocks),
        in_specs=[q_spec, seq_spec, seq_spec, ctx_spec, ctx_spec, bias_spec],
        out_specs=q_spec,
        out_shape=jax.ShapeDtypeStruct((bsz, n_tok, d), BF16),
        compiler_params=_cparams(("arbitrary", "arbitrary", "arbitrary")),
        name="nbr_attention",
    )(q, k, v, kc, vc, bias_tabs)


def _ctx_attn_kernel(q_ref, k_ref, v_ref, o_ref):
    head_dim = q_ref.shape[1] // 2
    q = q_ref[...]
    k = k_ref[...]
    v = v_ref[...]
    lane = lax.broadcasted_iota(jnp.int32, q.shape, 1)
    in_head0 = lane < head_dim
    nt = (((1,), (1,)), ((), ()))

    def head(qh):
        s = lax.dot_general(qh, k, nt, preferred_element_type=F32)
        p = jnp.exp(s - jnp.max(s, axis=-1, keepdims=True))
        o = jnp.dot(p.astype(BF16), v, preferred_element_type=F32)
        return o / jnp.sum(p, axis=-1, keepdims=True)

    o0 = head(jnp.where(in_head0, q, jnp.zeros_like(q)))
    o1 = head(jnp.where(in_head0, jnp.zeros_like(q), q))
    o_ref[...] = jnp.where(in_head0, o0, o1).astype(o_ref.dtype)


def _ctx_attention(qc, kc, vc):
    bsz, n_ctx, d = qc.shape
    pair = 2 * (d // NA_HEADS)
    spec = pl.BlockSpec((None, n_ctx, pair), lambda b, hp: (b, 0, hp))
    return pl.pallas_call(
        _ctx_attn_kernel,
        grid=(bsz, d // pair),
        in_specs=[spec, spec, spec],
        out_specs=spec,
        out_shape=jax.ShapeDtypeStruct((bsz, n_ctx, d), BF16),
        compiler_params=_cparams(("arbitrary", "arbitrary")),
        name="ctx_attention",
    )(qc, kc, vc)


def _gelu_tanh(x):
    c = float(np.sqrt(2.0 / np.pi))
    return 0.5 * x * (1.0 + jnp.tanh(c * (x + 0.044715 * (x * x * x))))


def _lru_in_kernel(x_ref, xp_ref, xn_ref, mod_ref, npre_ref, wg_ref, wr_ref, cw_ref, cb_ref,
                   u_ref, g_ref):
    tm = x_ref.shape[0]
    halo = xp_ref.shape[0]
    i = pl.program_id(1)
    last = pl.num_programs(1) - 1
    mod = mod_ref[...]
    npre = npre_ref[...]
    x_ext = jnp.concatenate([xp_ref[...], x_ref[...], xn_ref[...]], axis=0)
    h_ext = _modulate(x_ext, npre, mod, 1)
    rec = jnp.dot(h_ext.astype(BF16), wr_ref[...], preferred_element_type=F32)
    row = lax.broadcasted_iota(jnp.int32, (tm + 2 * halo, 1), 0)
    valid = ((row >= halo) | (i > 0)) & ((row < halo + tm) | (i < last))
    rec = jnp.where(valid, rec, 0.0)
    cw = cw_ref[...]
    n_ext = tm + 2 * halo
    u = None
    for tap in range(CONV_W):
        shift = (CONV_PAD_LEFT - tap) % n_ext
        shifted = rec if shift == 0 else pltpu.roll(rec, shift, 0)
        term = shifted[halo:halo + tm] * cw[tap:tap + 1]
        u = term + cb_ref[...] if u is None else u + term
    u_ref[...] = u
    hm = h_ext[halo:halo + tm].astype(BF16)
    g_ref[...] = _gelu_tanh(jnp.dot(hm, wg_ref[...], preferred_element_type=F32)).astype(g_ref.dtype)


def _lru_in(x3d, mod, npre, w_gate, w_rec, conv_w, conv_b):
    bsz, n_tok, d = x3d.shape
    width = w_rec.shape[1]
    tm = min(PROJ_ROWS, n_tok)
    halo = SUBLANES
    halo_blocks = n_tok // halo
    per_tile = tm // halo
    if mod.shape[0] > 1:
        mod_map = lambda b, i: (b, 0, 0)
    else:
        mod_map = lambda b, i: (0, 0, 0)
    out_spec = pl.BlockSpec((None, tm, width), lambda b, i: (b, i, 0))
    return pl.pallas_call(
        _lru_in_kernel,
        grid=(bsz, n_tok // tm),
        in_specs=[
            pl.BlockSpec((None, tm, d), lambda b, i: (b, i, 0)),
            pl.BlockSpec((None, halo, d), lambda b, i: (b, jnp.maximum(i * per_tile - 1, 0), 0)),
            pl.BlockSpec((None, halo, d),
                         lambda b, i: (b, jnp.minimum((i + 1) * per_tile, halo_blocks - 1), 0)),
            pl.BlockSpec((None,) + mod.shape[1:], mod_map),
            _const_spec(npre.shape), _const_spec(w_gate.shape), _const_spec(w_rec.shape),
            _const_spec(conv_w.shape), _const_spec((1, width)),
        ],
        out_specs=[out_spec, out_spec],
        out_shape=[jax.ShapeDtypeStruct((bsz, n_tok, width), F32),
                   jax.ShapeDtypeStruct((bsz, n_tok, width), BF16)],
        compiler_params=_cparams(("arbitrary", "arbitrary")),
        name="lru_in",
    )(x3d, x3d, x3d, mod, npre, w_gate, w_rec, conv_w, conv_b.reshape(1, width))


def _scan_kernel(*refs, reverse, combine):
    if combine:
        (u_ref, wcat_ref, ba_ref, bx_ref, lam_ref, init_ref, prev_ref, g_ref,
         o_ref, fin_ref, a_s, b_s, carry_s) = refs
    else:
        (u_ref, wcat_ref, ba_ref, bx_ref, lam_ref, init_ref,
         o_ref, fin_ref, a_s, b_s, carry_s) = refs
    tt, width = u_ref.shape
    n_blk, blk_w = wcat_ref.shape[0], wcat_ref.shape[1]
    n_tiles = tt // SUBLANES

    @pl.when(pl.program_id(1) == 0)
    def _():
        carry_s[...] = init_ref[...]

    lam = lam_ref[...]
    neg_c_softplus = -LRU_C * (jnp.maximum(-lam, 0.0) + jnp.log1p(jnp.exp(-jnp.abs(lam))))
    sub = lax.broadcasted_iota(jnp.int32, (n_tiles, SUBLANES, blk_w), 1)

    for n in range(n_blk):
        cols = slice(n * blk_w, (n + 1) * blk_w)
        un = u_ref[:, cols]
        z = jnp.dot(un.astype(BF16), wcat_ref[n], preferred_element_type=F32)
        r = jax.nn.sigmoid(z[:, :blk_w] + ba_ref[:, cols])
        ig = jax.nn.sigmoid(z[:, blk_w:] + bx_ref[:, cols])
        log_a = r * neg_c_softplus[:, cols]
        a = jnp.exp(log_a)
        b = jnp.sqrt(1.0 - a * a) * (ig * un)
        a3 = a.reshape(n_tiles, SUBLANES, blk_w)
        b3 = b.reshape(n_tiles, SUBLANES, blk_w)
        dist = 1
        while dist < SUBLANES:
            if reverse:
                keep = sub < SUBLANES - dist
                shift = SUBLANES - dist
            else:
                keep = sub >= dist
                shift = dist
            a_nb = jnp.where(keep, pltpu.roll(a3, shift, 1), 1.0)
            b_nb = jnp.where(keep, pltpu.roll(b3, shift, 1), 0.0)
            b3 = b3 + a3 * b_nb
            a3 = a3 * a_nb
            dist *= 2
        a_s[:, cols] = a3.reshape(tt, blk_w)
        b_s[:, cols] = b3.reshape(tt, blk_w)

    edge = 0 if reverse else SUBLANES - 1

    def tile_step(t, carry):
        tile = (n_tiles - 1 - t) if reverse else t
        rows = pl.ds(pl.multiple_of(tile * SUBLANES, SUBLANES), SUBLANES)
        h = b_s[rows, :] + a_s[rows, :] * carry
        b_s[rows, :] = h
        return h[edge:edge + 1, :]

    carry = lax.fori_loop(0, n_tiles, tile_step, carry_s[...], unroll=4)
    carry_s[...] = carry
    fin_ref[...] = carry
    if combine:
        o_ref[...] = ((b_s[...] + prev_ref[...]) * g_ref[...].astype(F32)).astype(o_ref.dtype)
    else:
        o_ref[...] = b_s[...]


def _scan(u, wcat, b_a, b_x, lam, init, reverse, prev=None, gate=None):
    bsz, n_tok, width = u.shape
    tt = min(SCAN_ROWS, n_tok)
    n_chunks = n_tok // tt
    combine = prev is not None
    if reverse:
        seq_map = lambda b, i: (b, n_chunks - 1 - i, 0)
    else:
        seq_map = lambda b, i: (b, i, 0)
    seq_spec = pl.BlockSpec((None, tt, width), seq_map)
    vec_spec = pl.BlockSpec((None, 1, width), lambda b, i: (b, 0, 0))
    in_specs = [seq_spec, _const_spec(wcat.shape), _const_spec((1, width)), _const_spec((1, width)),
                _const_spec((1, width)), vec_spec]
    args = [u, wcat, b_a.reshape(1, width), b_x.reshape(1, width), lam.reshape(1, width), init]
    if combine:
        in_specs += [seq_spec, seq_spec]
        args += [prev, gate]
    return pl.pallas_call(
        functools.partial(_scan_kernel, reverse=reverse, combine=combine),
        grid=(bsz, n_chunks),
        in_specs=in_specs,
        out_specs=[seq_spec, vec_spec],
        out_shape=[jax.ShapeDtypeStruct((bsz, n_tok, width), BF16 if combine else F32),
                   jax.ShapeDtypeStruct((bsz, 1, width), F32)],
        scratch_shapes=[pltpu.VMEM((tt, width), F32), pltpu.VMEM((tt, width), F32),
                        pltpu.VMEM((1, width), F32)],
        compiler_params=_cparams(("arbitrary", "arbitrary")),
        name="lru_scan_bwd" if reverse else "lru_scan_fwd",
    )(*args)


def _bidirectional_lru(u, gate, wcat, b_a, b_x, lam, init_f, init_b):
    h_f, fin_f = _scan(u, wcat[0], b_a[0], b_x[0], lam[0], init_f, reverse=False)
    act, fin_b = _scan(u, wcat[1], b_a[1], b_x[1], lam[1], init_b, reverse=True, prev=h_f, gate=gate)
    return act, fin_f, fin_b


def _rope_tables(n_tok, head_dim):
    pairs = head_dim // 4
    t = jnp.arange(n_tok, dtype=jnp.int32)
    row = (t // GRID_W).astype(F32)
    col = (t % GRID_W).astype(F32)
    inv_freq = jnp.power(ROPE_BASE, -jnp.arange(pairs, dtype=F32) / pairs)
    ang = jnp.concatenate([row[:, None] * inv_freq, col[:, None] * inv_freq], axis=-1)
    cos, sin = jnp.cos(ang), jnp.sin(ang)
    cos_h = jnp.concatenate([cos, cos], axis=-1)
    sin_h = jnp.concatenate([-sin, sin], axis=-1)
    return jnp.tile(cos_h, (1, 2)), jnp.tile(sin_h, (1, 2))


def kernel(x, c, ctx, c_ctx, ada_w, ada_b, norm_pre, norm_post, ffn_w_gate, ffn_w_up, ffn_w_down,
           na_w_qkv, na_w_o, na_rpb, lru_w_in, lru_conv_w, lru_conv_b, lru_w_a, lru_b_a, lru_w_x,
           lru_b_x, lru_lambda, lru_w_o):
    bsz, n_tok, d = x.shape
    n_ctx = ctx.shape[1]
    depth = ada_w.shape[0]
    n_mod = ada_w.shape[2] // d
    head_dim = d // NA_HEADS
    lru_width = lru_w_o.shape[1]
    n_mixers = 2

    cond = jnp.concatenate([c, c_ctx[None, :], jnp.zeros((SUBLANES - bsz - 1, d), F32)], axis=0)
    mods = _adaln(cond, ada_w, ada_b).reshape(depth, SUBLANES, n_mod, d)

    cos_t, sin_t = _rope_tables(n_tok, head_dim)
    ones_t = jnp.ones((n_ctx, cos_t.shape[1]), F32)
    zeros_t = jnp.zeros((n_ctx, cos_t.shape[1]), F32)

    wg_all = ffn_w_gate.astype(BF16)
    wu_all = ffn_w_up.astype(BF16)
    wd_all = ffn_w_down.astype(BF16)

    xl = x.reshape(bsz * n_tok, d)
    xc = ctx.reshape(bsz * n_ctx, d)
    for i in range(depth):
        last = i == depth - 1
        j = i // n_mixers
        mod = mods[i, :bsz]
        mod_c = mods[i, bsz:bsz + 1]
        npre, npost = norm_pre[i], norm_post[i]

        def half_ffn(u2d, rows, m, s, mix=None):
            f = s // 2
            return _ffn(u2d, rows, m, npre, npost, wg_all[i, f], wu_all[i, f], wd_all[i, f], s, mix)

        xl = half_ffn(xl, n_tok, mod, 0)
        xc = half_ffn(xc, n_ctx, mod_c, 0)
        if i % n_mixers == 0:
            w_qkv = na_w_qkv[j].astype(BF16)
            w_o = na_w_o[j].astype(BF16)
            q, k, v = _qkv(xl, n_tok, mod, npre, cos_t, sin_t, w_qkv)
            qc, kc, vc = _qkv(xc, n_ctx, mod_c, npre, ones_t, zeros_t, w_qkv)
            shp, shp_c = (bsz, n_tok, d), (bsz, n_ctx, d)
            kc3, vc3 = kc.reshape(shp_c), vc.reshape(shp_c)
            bias_tabs = _attn_bias_tables(na_rpb[j], n_tok // GRID_W)
            act = _attention(q.reshape(shp), k.reshape(shp), v.reshape(shp), kc3, vc3, bias_tabs)
            act = act.reshape(bsz * n_tok, d)
            if not last:
                act_c = _ctx_attention(qc.reshape(shp_c), kc3, vc3).reshape(bsz * n_ctx, d)
        else:
            w_in = lru_w_in[j].astype(BF16)
            w_gate, w_rec = w_in[:, :lru_width], w_in[:, lru_width:]
            w_o = lru_w_o[j].astype(BF16)
            wcat = jnp.concatenate([lru_w_a[j], lru_w_x[j]], axis=-1).astype(BF16)
            u_l, g_l = _lru_in(xl.reshape(bsz, n_tok, d), mod, npre, w_gate, w_rec,
                               lru_conv_w[j], lru_conv_b[j])
            u_c, g_c = _lru_in(xc.reshape(bsz, n_ctx, d), mod_c, npre, w_gate, w_rec,
                               lru_conv_w[j], lru_conv_b[j])
            zero_state = jnp.zeros((bsz, 1, lru_width), F32)
            act_c, fin_f, fin_b = _bidirectional_lru(u_c, g_c, wcat, lru_b_a[j], lru_b_x[j],
                                                     lru_lambda[j], zero_state, zero_state)
            act, _, _ = _bidirectional_lru(u_l, g_l, wcat, lru_b_a[j], lru_b_x[j],
                                           lru_lambda[j], fin_f, fin_b)
            act = act.reshape(bsz * n_tok, lru_width)
            act_c = act_c.reshape(bsz * n_ctx, lru_width)
        xl = half_ffn(xl, n_tok, mod, 2, mix=(act, w_o))
        if not last:
            xc = half_ffn(xc, n_ctx, mod_c, 2, mix=(act_c, w_o))
    return xl.reshape(bsz, n_tok, d)
```

```python
import functools

import numpy as np
import jax
import jax.numpy as jnp
from jax import lax
from jax.experimental import pallas as pl
from jax.experimental.pallas import tpu as pltpu

F32 = jnp.float32
BF16 = jnp.bfloat16

GRID_W = 64
NA_HEADS = 16
NA_WIN_ROWS = 8
NA_WIN_COLS = 16
ROPE_BASE = 10000.0
LRU_C = 8.0
CONV_W = 4
CONV_PAD_LEFT = 2
RMS_EPS = 1e-6
FFN_RES_WEIGHT = 0.5

LANES = 128
SUBLANES = 8
MXU_DIM = 256

FFN_ROWS = 512
FFN_HID_CHUNK = MXU_DIM
PROJ_ROWS = 512
ATT_ROWS = 4
ATT_PAIRS = 2
SCAN_ROWS = 512
MOD_COLS = 1024
VMEM_LIMIT = 56 * 1024 * 1024
NEG_BIG = -1e30
SQRT_TINY = 1e-30
LOG2_E = float(np.log2(np.e))


def _cparams(sem):
    return pltpu.CompilerParams(dimension_semantics=sem, vmem_limit_bytes=VMEM_LIMIT)


def _const_spec(shape):
    nd = len(shape)
    return pl.BlockSpec(shape, lambda *_: (0,) * nd, pipeline_mode=pl.Buffered(1))


def _rms(x, gain):
    ms = jnp.mean(x * x, axis=-1, keepdims=True)
    return x * lax.rsqrt(ms + RMS_EPS) * gain


def _modulate(x, npre, mod, s):
    return _rms(x, npre[s:s + 1]) * (1.0 + mod[3 * s + 1:3 * s + 2]) + mod[3 * s:3 * s + 1]


def _mod_kernel(c_ref, w_ref, b_ref, o_ref):
    c = c_ref[...]
    s = (c * jax.nn.sigmoid(c)).astype(BF16)
    o_ref[...] = jnp.dot(s, w_ref[...].astype(BF16), preferred_element_type=F32) + b_ref[...]


def _adaln(cond, ada_w, ada_b):
    depth, d, n = ada_w.shape
    rows = cond.shape[0]
    return pl.pallas_call(
        _mod_kernel,
        grid=(depth, n // MOD_COLS),
        in_specs=[
            pl.BlockSpec((rows, d), lambda l, j: (0, 0)),
            pl.BlockSpec((None, d, MOD_COLS), lambda l, j: (l, 0, j)),
            pl.BlockSpec((None, 1, MOD_COLS), lambda l, j: (l, 0, j)),
        ],
        out_specs=pl.BlockSpec((None, rows, MOD_COLS), lambda l, j: (l, 0, j)),
        out_shape=jax.ShapeDtypeStruct((depth, rows, n), F32),
        compiler_params=_cparams(("arbitrary", "arbitrary")),
        name="adaln_mod",
    )(cond, ada_w, ada_b.reshape(depth, 1, n))


def _ffn_kernel(*refs, s, has_mix):
    if has_mix:
        x_ref, a_ref, wo_ref, mod_ref, npre_ref, npost_ref, wg_ref, wu_ref, wd_ref, o_ref = refs
    else:
        x_ref, mod_ref, npre_ref, npost_ref, wg_ref, wu_ref, wd_ref, o_ref = refs
    x = x_ref[...]
    mod = mod_ref[...]
    npre = npre_ref[...]
    npost = npost_ref[...]
    if has_mix:
        y = jnp.dot(a_ref[...], wo_ref[...], preferred_element_type=F32)
        x = x + mod[5:6] * _rms(y, npost[1:2])
    h = _modulate(x, npre, mod, s).astype(BF16)
    hid = wg_ref.shape[1]
    acc = None
    for c in range(0, hid, FFN_HID_CHUNK):
        g = jnp.dot(h, wg_ref[:, c:c + FFN_HID_CHUNK], preferred_element_type=F32)
        u = jnp.dot(h, wu_ref[:, c:c + FFN_HID_CHUNK], preferred_element_type=F32)
        a = (g * jax.nn.sigmoid(g) * u).astype(BF16)
        part = jnp.dot(a, wd_ref[c:c + FFN_HID_CHUNK, :], preferred_element_type=F32)
        acc = part if acc is None else acc + part
    o_ref[...] = x + FFN_RES_WEIGHT * mod[3 * s + 2:3 * s + 3] * _rms(acc, npost[s:s + 1])


def _ffn(x2d, rows_per_batch, mod, npre, npost, wg, wu, wd, s, mix=None):
    n, d = x2d.shape
    tm = min(FFN_ROWS, n)
    if mod.shape[0] > 1:
        mod_map = lambda i: ((i * tm) // rows_per_batch, 0, 0)
    else:
        mod_map = lambda i: (0, 0, 0)
    row_spec = pl.BlockSpec((tm, d), lambda i: (i, 0))
    in_specs = [row_spec]
    args = [x2d]
    if mix is not None:
        a2d, wo = mix
        in_specs += [pl.BlockSpec((tm, a2d.shape[1]), lambda i: (i, 0)), _const_spec(wo.shape)]
        args += [a2d, wo]
    in_specs += [pl.BlockSpec((None,) + mod.shape[1:], mod_map), _const_spec(npre.shape),
                 _const_spec(npost.shape), _const_spec(wg.shape), _const_spec(wu.shape),
                 _const_spec(wd.shape)]
    args += [mod, npre, npost, wg, wu, wd]
    return pl.pallas_call(
        functools.partial(_ffn_kernel, s=s, has_mix=mix is not None),
        grid=(n // tm,),
        in_specs=in_specs,
        out_specs=row_spec,
        out_shape=jax.ShapeDtypeStruct((n, d), F32),
        compiler_params=_cparams(("arbitrary",)),
        name="mix_ffn" if mix is not None else "ffn",
    )(*args)


def _qkv_kernel(x_ref, mod_ref, npre_ref, cos_ref, sin_ref, w_ref, q_ref, k_ref, v_ref, *, scale):
    d = x_ref.shape[1]
    half = d // NA_HEADS // 2
    h = _modulate(x_ref[...], npre_ref[...], mod_ref[...], 1).astype(BF16)
    reps = d // cos_ref.shape[1]
    cosf = jnp.tile(cos_ref[...], (1, reps))
    sinf = jnp.tile(sin_ref[...], (1, reps))
    lane = lax.broadcasted_iota(jnp.int32, (x_ref.shape[0], d), 1)
    first = (lane & (2 * half - 1)) < half

    def rope(t):
        partner = jnp.where(first, pltpu.roll(t, d - half, 1), pltpu.roll(t, half, 1))
        return t * cosf + partner * sinf

    q = jnp.dot(h, w_ref[:, 0:d], preferred_element_type=F32)
    q_ref[...] = (rope(q) * scale).astype(BF16)
    k = jnp.dot(h, w_ref[:, d:2 * d], preferred_element_type=F32)
    k_ref[...] = rope(k).astype(BF16)
    v = jnp.dot(h, w_ref[:, 2 * d:3 * d], preferred_element_type=F32)
    v_ref[...] = v.astype(BF16)


def _qkv(x2d, rows_per_batch, mod, npre, cos_t, sin_t, w_qkv):
    n, d = x2d.shape
    tm = min(PROJ_ROWS, rows_per_batch)
    tiles_per_batch = rows_per_batch // tm
    if mod.shape[0] > 1:
        mod_map = lambda i: (i // tiles_per_batch, 0, 0)
    else:
        mod_map = lambda i: (0, 0, 0)
    row_spec = pl.BlockSpec((tm, d), lambda i: (i, 0))
    tab_spec = pl.BlockSpec((tm, cos_t.shape[1]), lambda i: (i % tiles_per_batch, 0))
    out = jax.ShapeDtypeStruct((n, d), BF16)
    return pl.pallas_call(
        functools.partial(_qkv_kernel, scale=float((d // NA_HEADS) ** -0.5) * LOG2_E),
        grid=(n // tm,),
        in_specs=[row_spec, pl.BlockSpec((None,) + mod.shape[1:], mod_map), _const_spec(npre.shape),
                  tab_spec, tab_spec, _const_spec(w_qkv.shape)],
        out_specs=[row_spec, row_spec, row_spec],
        out_shape=[out, out, out],
        compiler_params=_cparams(("arbitrary",)),
        name="qkv_rope",
    )(x2d, mod, npre, cos_t, sin_t, w_qkv)


_NT = (((1,), (1,)), ((), ()))


def _lane_is_head0(shape):
    return lax.broadcasted_iota(jnp.int32, shape, len(shape) - 1) < shape[-1] // 2


def _head_scores(q, hd, kv_blocks, bias):
    is0 = _lane_is_head0(q.shape)
    qh = jnp.where(is0 if hd == 0 else jnp.logical_not(is0), q, jnp.zeros_like(q))
    scores = []
    for blk, (k, _) in enumerate(kv_blocks):
        s = lax.dot_general(qh, k, _NT, preferred_element_type=F32)
        if blk == 0 and bias is not None:
            s = s + bias
        scores.append(s)
    return scores


def _head_output(scores, hd, kv_blocks):
    m = functools.reduce(jnp.maximum, [jnp.max(s, axis=-1, keepdims=True) for s in scores])
    o = None
    for s, (_, v) in zip(scores, kv_blocks):
        is0 = _lane_is_head0(v.shape)
        v_aug = jnp.where(is0 if hd == 0 else jnp.logical_not(is0), v, jnp.ones_like(v))
        part = jnp.dot(jnp.exp2(s - m).astype(BF16), v_aug, preferred_element_type=F32)
        o = part if o is None else o + part
    return o


def _attend(jobs):
    raw = []
    pending = None
    for q, hd, kv_blocks, bias in jobs:
        scores = _head_scores(q, hd, kv_blocks, bias)
        if pending is not None:
            raw.append(_head_output(*pending))
        pending = (scores, hd, kv_blocks)
    raw.append(_head_output(*pending))
    outs = []
    for o0, o1 in zip(raw[0::2], raw[1::2]):
        is0 = _lane_is_head0(o0.shape)
        num = jnp.where(is0, o0, o1)
        den = pltpu.roll(jnp.where(is0, o1, o0), o0.shape[-1] // 2, 1)
        outs.append(num / den)
    return outs


def _attn_kernel(q_ref, k_ref, v_ref, kc_ref, vc_ref, bias_ref, o_ref, *, n_rows):
    nk = bias_ref.shape[-1]
    rb = pl.program_id(2)
    w0 = jnp.clip(rb * ATT_ROWS - NA_WIN_ROWS // 2, 0, n_rows - nk // GRID_W)
    start = pl.multiple_of(w0 * GRID_W, GRID_W)
    jobs = []
    for pp in range(bias_ref.shape[0]):
        lanes = slice(pp * LANES, (pp + 1) * LANES)
        kv_blocks = [(k_ref[pl.ds(start, nk), lanes], v_ref[pl.ds(start, nk), lanes]),
                     (kc_ref[:, lanes], vc_ref[:, lanes])]
        q = q_ref[:, lanes]
        jobs += [(q, hd, kv_blocks, bias_ref[pp, hd]) for hd in range(2)]
    for pp, o in enumerate(_attend(jobs)):
        o_ref[:, pp * LANES:(pp + 1) * LANES] = o.astype(o_ref.dtype)


def _attn_bias_tables(rpb, n_rows):
    heads, n_dr, n_dc = rpb.shape
    win_rows = min(NA_WIN_ROWS, n_rows)
    win_cols = min(NA_WIN_COLS, GRID_W)
    key_rows = ATT_ROWS + NA_WIN_ROWS - 1
    n_blocks = n_rows // ATT_ROWS
    lead = GRID_W - NA_WIN_COLS
    padded = jnp.pad(rpb.astype(F32) * LOG2_E, ((0, 0), (0, 0), (lead, 2 * GRID_W - 1 - lead - n_dc)))
    band = jnp.stack([padded[:, :, GRID_W - 1 - qc:2 * GRID_W - 1 - qc] for qc in range(GRID_W)], axis=2)
    col = np.arange(GRID_W)
    c0 = np.clip(col - win_cols // 2, 0, GRID_W - win_cols)
    valid_c = (col[None, :] >= c0[:, None]) & (col[None, :] < c0[:, None] + win_cols)
    band = jnp.where(jnp.asarray(valid_c)[None, None], band, NEG_BIG)
    masked = jnp.full((heads, GRID_W, GRID_W), NEG_BIG, F32)
    tabs = []
    for blk in (0, 1, n_blocks - 1):
        r_b = blk * ATT_ROWS
        w0 = int(np.clip(r_b - NA_WIN_ROWS // 2, 0, n_rows - key_rows))
        row_slabs = []
        for ri in range(ATT_ROWS):
            r = r_b + ri
            r0 = int(np.clip(r - win_rows // 2, 0, n_rows - win_rows))
            tiles = []
            for kri in range(key_rows):
                kr = w0 + kri
                if r0 <= kr < r0 + win_rows:
                    tiles.append(band[:, kr - r + NA_WIN_ROWS - 1])
                else:
                    tiles.append(masked)
            row_slabs.append(jnp.concatenate(tiles, axis=-1))
        tabs.append(jnp.concatenate(row_slabs, axis=1))
    t = jnp.stack(tabs, axis=0)
    return t.reshape(3, heads // 2, 2, t.shape[-2], t.shape[-1])


def _attention(q, k, v, kc, vc, bias_tabs):
    bsz, n_tok, d = q.shape
    n_ctx = kc.shape[1]
    n_rows = n_tok // GRID_W
    n_blocks = n_rows // ATT_ROWS
    rq = ATT_ROWS * GRID_W
    pair = ATT_PAIRS * LANES
    q_spec = pl.BlockSpec((None, rq, pair), lambda b, hp, rb: (b, rb, hp))
    seq_spec = pl.BlockSpec((None, n_tok, pair), lambda b, hp, rb: (b, 0, hp))
    ctx_spec = pl.BlockSpec((None, n_ctx, pair), lambda b, hp, rb: (b, 0, hp))

    def bias_map(b, hp, rb):
        kind = jnp.where(rb == 0, 0, jnp.where(rb == n_blocks - 1, 2, 1))
        return (kind, hp, 0, 0, 0, 0)

    bias_tabs = bias_tabs.reshape((3, d // pair, ATT_PAIRS) + bias_tabs.shape[2:])
    bias_spec = pl.BlockSpec((None, None) + bias_tabs.shape[2:], bias_map)
    return pl.pallas_call(
        functools.partial(_attn_kernel, n_rows=n_rows),
        grid=(bsz, d // pair, n_blocks),
        in_specs=[q_spec, seq_spec, seq_spec, ctx_spec, ctx_spec, bias_spec],
        out_specs=q_spec,
        out_shape=jax.ShapeDtypeStruct((bsz, n_tok, d), BF16),
        compiler_params=_cparams(("arbitrary", "arbitrary", "arbitrary")),
        name="nbr_attention",
    )(q, k, v, kc, vc, bias_tabs)


def _ctx_attn_kernel(q_ref, k_ref, v_ref, o_ref):
    q = q_ref[...]
    kv_blocks = [(k_ref[...], v_ref[...])]
    (o,) = _attend([(q, hd, kv_blocks, None) for hd in range(2)])
    o_ref[...] = o.astype(o_ref.dtype)


def _ctx_attention(qc, kc, vc):
    bsz, n_ctx, d = qc.shape
    pair = 2 * (d // NA_HEADS)
    spec = pl.BlockSpec((None, n_ctx, pair), lambda b, hp: (b, 0, hp))
    return pl.pallas_call(
        _ctx_attn_kernel,
        grid=(bsz, d // pair),
        in_specs=[spec, spec, spec],
        out_specs=spec,
        out_shape=jax.ShapeDtypeStruct((bsz, n_ctx, d), BF16),
        compiler_params=_cparams(("arbitrary", "arbitrary")),
        name="ctx_attention",
    )(qc, kc, vc)


def _gelu_tanh(x):
    c = float(np.sqrt(2.0 / np.pi))
    return 0.5 * x * (1.0 + jnp.tanh(c * (x + 0.044715 * (x * x * x))))


def _lru_in_kernel(x_ref, xp_ref, xn_ref, mod_ref, npre_ref, wg_ref, wr_ref, cw_ref, cb_ref,
                   u_ref, g_ref):
    tm = x_ref.shape[0]
    halo = xp_ref.shape[0]
    i = pl.program_id(1)
    last = pl.num_programs(1) - 1
    mod = mod_ref[...]
    npre = npre_ref[...]
    x_ext = jnp.concatenate([xp_ref[...], x_ref[...], xn_ref[...]], axis=0)
    h_ext = _modulate(x_ext, npre, mod, 1)
    rec = jnp.dot(h_ext.astype(BF16), wr_ref[...], preferred_element_type=F32)
    row = lax.broadcasted_iota(jnp.int32, (tm + 2 * halo, 1), 0)
    valid = ((row >= halo) | (i > 0)) & ((row < halo + tm) | (i < last))
    rec = jnp.where(valid, rec, 0.0)
    cw = cw_ref[...]
    n_ext = tm + 2 * halo
    u = None
    for tap in range(CONV_W):
        shift = (CONV_PAD_LEFT - tap) % n_ext
        shifted = rec if shift == 0 else pltpu.roll(rec, shift, 0)
        term = shifted[halo:halo + tm] * cw[tap:tap + 1]
        u = term + cb_ref[...] if u is None else u + term
    n_blk, _, blk_w = u_ref.shape
    seg = tm // SUBLANES
    for n in range(n_blk):
        for j in range(SUBLANES):
            u_ref[n, pl.ds(j, seg, stride=SUBLANES), :] = u[j * seg:(j + 1) * seg,
                                                           n * blk_w:(n + 1) * blk_w]
    hm = h_ext[halo:halo + tm].astype(BF16)
    g_ref[...] = _gelu_tanh(jnp.dot(hm, wg_ref[...], preferred_element_type=F32)).astype(g_ref.dtype)


def _lru_in(x3d, mod, npre, w_gate, w_rec, conv_w, conv_b):
    bsz, n_tok, d = x3d.shape
    width = w_rec.shape[1]
    tm = min(SCAN_ROWS, n_tok)
    n_blk = width // LANES
    halo = SUBLANES
    halo_blocks = n_tok // halo
    per_tile = tm // halo
    if mod.shape[0] > 1:
        mod_map = lambda b, i: (b, 0, 0)
    else:
        mod_map = lambda b, i: (0, 0, 0)
    out_spec = pl.BlockSpec((None, tm, width), lambda b, i: (b, i, 0))
    u_spec = pl.BlockSpec((None, None, n_blk, tm, LANES), lambda b, i: (b, i, 0, 0, 0))
    return pl.pallas_call(
        _lru_in_kernel,
        grid=(bsz, n_tok // tm),
        in_specs=[
            pl.BlockSpec((None, tm, d), lambda b, i: (b, i, 0)),
            pl.BlockSpec((None, halo, d), lambda b, i: (b, jnp.maximum(i * per_tile - 1, 0), 0)),
            pl.BlockSpec((None, halo, d),
                         lambda b, i: (b, jnp.minimum((i + 1) * per_tile, halo_blocks - 1), 0)),
            pl.BlockSpec((None,) + mod.shape[1:], mod_map),
            _const_spec(npre.shape), _const_spec(w_gate.shape), _const_spec(w_rec.shape),
            _const_spec(conv_w.shape), _const_spec((1, width)),
        ],
        out_specs=[u_spec, out_spec],
        out_shape=[jax.ShapeDtypeStruct((bsz, n_tok // tm, n_blk, tm, LANES), F32),
                   jax.ShapeDtypeStruct((bsz, n_tok, width), BF16)],
        compiler_params=_cparams(("arbitrary", "arbitrary")),
        name="lru_in",
    )(x3d, x3d, x3d, mod, npre, w_gate, w_rec, conv_w, conv_b.reshape(1, width))


def _scan_kernel(*refs, reverse, combine):
    if combine:
        (u_ref, wcat_ref, ba_ref, bx_ref, lam_ref, init_ref, prev_ref, g_ref,
         o_ref, fin_ref, h_s, p_s, carry_s) = refs
    else:
        (u_ref, wcat_ref, ba_ref, bx_ref, lam_ref, init_ref,
         o_ref, fin_ref, h_s, p_s, carry_s) = refs
    n_blk, tt, blk_w = u_ref.shape
    seg = tt // SUBLANES

    @pl.when(pl.program_id(1) == 0)
    def _():
        carry_s[...] = init_ref[...]

    lam = lam_ref[...]
    softplus = jnp.maximum(-lam, 0.0) + jnp.log1p(jnp.exp(-jnp.abs(lam)))
    half_log2_decay = (-0.5 * LRU_C * LOG2_E) * softplus
    segments = range(SUBLANES - 1, -1, -1) if reverse else range(SUBLANES)

    for n in range(n_blk):
        cols = slice(n * blk_w, (n + 1) * blk_w)
        un = u_ref[n]
        z = jnp.dot(un.astype(BF16), wcat_ref[n], preferred_element_type=F32)
        t_r = jnp.tanh(z[:, :blk_w] + ba_ref[:, cols])
        t_i = jnp.tanh(z[:, blk_w:] + bx_ref[:, cols])
        k = half_log2_decay[:, cols]
        a = jnp.exp2(k + k * t_r)
        one_m = 1.0 - a * a
        p_s[n] = a
        h_s[n] = (one_m * lax.rsqrt(jnp.maximum(one_m, SQRT_TINY))) * ((0.5 + 0.5 * t_i) * un)

    def scan_step(t, carry):
        i = (seg - 1 - t) if reverse else t
        tile = pl.ds(pl.multiple_of(i * SUBLANES, SUBLANES), SUBLANES)
        hs, ps = carry
        new_h, new_p = [], []
        for n in range(n_blk):
            a_t = p_s[n, tile, :]
            h = a_t * hs[n] + h_s[n, tile, :]
            p = a_t * ps[n]
            h_s[n, tile, :] = h
            p_s[n, tile, :] = p
            new_h.append(h)
            new_p.append(p)
        return tuple(new_h), tuple(new_p)

    zeros = tuple(jnp.zeros((SUBLANES, blk_w), F32) for _ in range(n_blk))
    ones = tuple(jnp.ones((SUBLANES, blk_w), F32) for _ in range(n_blk))
    h_end, p_end = lax.fori_loop(0, seg, scan_step, (zeros, ones), unroll=4)

    for n in range(n_blk):
        cols = slice(n * blk_w, (n + 1) * blk_w)
        h, p = h_end[n], p_end[n]
        state = carry_s[:, cols]
        entry = [None] * SUBLANES
        for j in segments:
            entry[j] = state
            state = h[j:j + 1] + p[j:j + 1] * state
        carry_s[:, cols] = state
        for j in range(SUBLANES):
            rows = slice(j * seg, (j + 1) * seg)
            picked = pl.ds(j, seg, stride=SUBLANES)
            ht = h_s[n, picked, :] + p_s[n, picked, :] * entry[j]
            if combine:
                ht = (ht + prev_ref[rows, cols]) * g_ref[rows, cols].astype(F32)
            o_ref[rows, cols] = ht.astype(o_ref.dtype)

    fin_ref[...] = carry_s[...]


def _scan(u, width, wcat, b_a, b_x, lam, init, reverse, prev=None, gate=None):
    bsz, n_chunks, n_blk, tt, blk_w = u.shape
    n_tok = n_chunks * tt
    combine = prev is not None
    if reverse:
        chunk = lambda i: n_chunks - 1 - i
    else:
        chunk = lambda i: i
    u_spec = pl.BlockSpec((None, None, n_blk, tt, blk_w), lambda b, i: (b, chunk(i), 0, 0, 0))
    seq_spec = pl.BlockSpec((None, tt, width), lambda b, i: (b, chunk(i), 0))
    vec_spec = pl.BlockSpec((None, 1, width), lambda b, i: (b, 0, 0))
    in_specs = [u_spec, _const_spec(wcat.shape), _const_spec((1, width)), _const_spec((1, width)),
                _const_spec((1, width)), vec_spec]
    args = [u, wcat, b_a.reshape(1, width), b_x.reshape(1, width), lam.reshape(1, width), init]
    if combine:
        in_specs += [seq_spec, seq_spec]
        args += [prev, gate]
    return pl.pallas_call(
        functools.partial(_scan_kernel, reverse=reverse, combine=combine),
        grid=(bsz, n_chunks),
        in_specs=in_specs,
        out_specs=[seq_spec, vec_spec],
        out_shape=[jax.ShapeDtypeStruct((bsz, n_tok, width), BF16 if combine else F32),
                   jax.ShapeDtypeStruct((bsz, 1, width), F32)],
        scratch_shapes=[pltpu.VMEM((n_blk, tt, blk_w), F32), pltpu.VMEM((n_blk, tt, blk_w), F32),
                        pltpu.VMEM((1, width), F32)],
        compiler_params=_cparams(("arbitrary", "arbitrary")),
        name="lru_scan_bwd" if reverse else "lru_scan_fwd",
    )(*args)


def _bidirectional_lru(u, gate, wcat, b_a, b_x, lam, init_f, init_b):
    width = gate.shape[-1]
    h_f, fin_f = _scan(u, width, wcat[0], b_a[0], b_x[0], lam[0], init_f, reverse=False)
    act, fin_b = _scan(u, width, wcat[1], b_a[1], b_x[1], lam[1], init_b, reverse=True,
                       prev=h_f, gate=gate)
    return act, fin_f, fin_b


def _rope_tables(n_tok, head_dim):
    pairs = head_dim // 4
    t = jnp.arange(n_tok, dtype=jnp.int32)
    row = (t // GRID_W).astype(F32)
    col = (t % GRID_W).astype(F32)
    inv_freq = jnp.power(ROPE_BASE, -jnp.arange(pairs, dtype=F32) / pairs)
    ang = jnp.concatenate([row[:, None] * inv_freq, col[:, None] * inv_freq], axis=-1)
    cos, sin = jnp.cos(ang), jnp.sin(ang)
    cos_h = jnp.concatenate([cos, cos], axis=-1)
    sin_h = jnp.concatenate([-sin, sin], axis=-1)
    return jnp.tile(cos_h, (1, 2)), jnp.tile(sin_h, (1, 2))


def kernel(x, c, ctx, c_ctx, ada_w, ada_b, norm_pre, norm_post, ffn_w_gate, ffn_w_up, ffn_w_down,
           na_w_qkv, na_w_o, na_rpb, lru_w_in, lru_conv_w, lru_conv_b, lru_w_a, lru_b_a, lru_w_x,
           lru_b_x, lru_lambda, lru_w_o):
    bsz, n_tok, d = x.shape
    n_ctx = ctx.shape[1]
    depth = ada_w.shape[0]
    n_mod = ada_w.shape[2] // d
    head_dim = d // NA_HEADS
    lru_width = lru_w_o.shape[1]
    n_mixers = 2

    cond = jnp.concatenate([c, c_ctx[None, :], jnp.zeros((SUBLANES - bsz - 1, d), F32)], axis=0)
    mods = _adaln(cond, ada_w, ada_b).reshape(depth, SUBLANES, n_mod, d)

    cos_t, sin_t = _rope_tables(n_tok, head_dim)
    ones_t = jnp.ones((n_ctx, cos_t.shape[1]), F32)
    zeros_t = jnp.zeros((n_ctx, cos_t.shape[1]), F32)

    wg_all = ffn_w_gate.astype(BF16)
    wu_all = ffn_w_up.astype(BF16)
    wd_all = ffn_w_down.astype(BF16)

    xl = x.reshape(bsz * n_tok, d)
    xc = ctx.reshape(bsz * n_ctx, d)
    for i in range(depth):
        last = i == depth - 1
        j = i // n_mixers
        mod = mods[i, :bsz]
        mod_c = mods[i, bsz:bsz + 1]
        npre, npost = norm_pre[i], norm_post[i]

        def half_ffn(u2d, rows, m, s, mix=None):
            f = s // 2
            return _ffn(u2d, rows, m, npre, npost, wg_all[i, f], wu_all[i, f], wd_all[i, f], s, mix)

        xl = half_ffn(xl, n_tok, mod, 0)
        xc = half_ffn(xc, n_ctx, mod_c, 0)
        if i % n_mixers == 0:
            w_qkv = na_w_qkv[j].astype(BF16)
            w_o = na_w_o[j].astype(BF16)
            q, k, v = _qkv(xl, n_tok, mod, npre, cos_t, sin_t, w_qkv)
            qc, kc, vc = _qkv(xc, n_ctx, mod_c, npre, ones_t, zeros_t, w_qkv)
            shp, shp_c = (bsz, n_tok, d), (bsz, n_ctx, d)
            kc3, vc3 = kc.reshape(shp_c), vc.reshape(shp_c)
            bias_tabs = _attn_bias_tables(na_rpb[j], n_tok // GRID_W)
            act = _attention(q.reshape(shp), k.reshape(shp), v.reshape(shp), kc3, vc3, bias_tabs)
            act = act.reshape(bsz * n_tok, d)
            if not last:
                act_c = _ctx_attention(qc.reshape(shp_c), kc3, vc3).reshape(bsz * n_ctx, d)
        else:
            w_in = lru_w_in[j].astype(BF16)
            w_gate, w_rec = w_in[:, :lru_width], w_in[:, lru_width:]
            w_o = lru_w_o[j].astype(BF16)
            wcat = (0.5 * jnp.concatenate([lru_w_a[j], lru_w_x[j]], axis=-1)).astype(BF16)
            half_b_a, half_b_x = 0.5 * lru_b_a[j], 0.5 * lru_b_x[j]
            u_l, g_l = _lru_in(xl.reshape(bsz, n_tok, d), mod, npre, w_gate, w_rec,
                               lru_conv_w[j], lru_conv_b[j])
            u_c, g_c = _lru_in(xc.reshape(bsz, n_ctx, d), mod_c, npre, w_gate, w_rec,
                               lru_conv_w[j], lru_conv_b[j])
            zero_state = jnp.zeros((bsz, 1, lru_width), F32)
            act_c, fin_f, fin_b = _bidirectional_lru(u_c, g_c, wcat, half_b_a, half_b_x,
                                                     lru_lambda[j], zero_state, zero_state)
            act, _, _ = _bidirectional_lru(u_l, g_l, wcat, half_b_a, half_b_x,
                                           lru_lambda[j], fin_f, fin_b)
            act = act.reshape(bsz * n_tok, lru_width)
            act_c = act_c.reshape(bsz * n_ctx, lru_width)
        xl = half_ffn(xl, n_tok, mod, 2, mix=(act, w_o))
        if not last:
            xc = half_ffn(xc, n_ctx, mod_c, 2, mix=(act_c, w_o))
    return xl.reshape(bsz, n_tok, d)
```

```python
import functools

import numpy as np
import jax
import jax.numpy as jnp
from jax import lax
from jax.experimental import pallas as pl
from jax.experimental.pallas import tpu as pltpu

F32 = jnp.float32
BF16 = jnp.bfloat16

GRID_W = 64
NA_HEADS = 16
NA_WIN_ROWS = 8
NA_WIN_COLS = 16
ROPE_BASE = 10000.0
LRU_C = 8.0
CONV_W = 4
CONV_PAD_LEFT = 2
RMS_EPS = 1e-6
FFN_RES_WEIGHT = 0.5

LANES = 128
SUBLANES = 8
MXU_DIM = 256

FFN_ROWS = 1024
FFN_SUB_ROWS = 512
FFN_HID_CHUNK = MXU_DIM
PROJ_ROWS = 512
ATT_ROWS = 4
ATT_PAIRS = 2
SCAN_ROWS = 512
MOD_COLS = 1024
VMEM_LIMIT = 56 * 1024 * 1024
NEG_BIG = -1e30
SQRT_TINY = 1e-30
LOG2_E = float(np.log2(np.e))


def _cparams(sem):
    return pltpu.CompilerParams(dimension_semantics=sem, vmem_limit_bytes=VMEM_LIMIT)


def _const_spec(shape):
    nd = len(shape)
    return pl.BlockSpec(shape, lambda *_: (0,) * nd, pipeline_mode=pl.Buffered(1))


def _rms(x, gain):
    ms = jnp.mean(x * x, axis=-1, keepdims=True)
    return x * lax.rsqrt(ms + RMS_EPS) * gain


def _modulate(x, npre, mod, s):
    return _rms(x, npre[s:s + 1]) * (1.0 + mod[3 * s + 1:3 * s + 2]) + mod[3 * s:3 * s + 1]


def _mod_kernel(c_ref, w_ref, b_ref, o_ref):
    c = c_ref[...]
    s = (c * jax.nn.sigmoid(c)).astype(BF16)
    o_ref[...] = jnp.dot(s, w_ref[...].astype(BF16), preferred_element_type=F32) + b_ref[...]


def _adaln(cond, ada_w, ada_b):
    depth, d, n = ada_w.shape
    rows = cond.shape[0]
    return pl.pallas_call(
        _mod_kernel,
        grid=(depth, n // MOD_COLS),
        in_specs=[
            pl.BlockSpec((rows, d), lambda l, j: (0, 0)),
            pl.BlockSpec((None, d, MOD_COLS), lambda l, j: (l, 0, j)),
            pl.BlockSpec((None, 1, MOD_COLS), lambda l, j: (l, 0, j)),
        ],
        out_specs=pl.BlockSpec((None, rows, MOD_COLS), lambda l, j: (l, 0, j)),
        out_shape=jax.ShapeDtypeStruct((depth, rows, n), F32),
        compiler_params=_cparams(("arbitrary", "arbitrary")),
        name="adaln_mod",
    )(cond, ada_w, ada_b.reshape(depth, 1, n))


def _ffn_kernel(*refs, s, has_mix):
    if has_mix:
        x_ref, a_ref, wo_ref, mod_ref, npre_ref, npost_ref, wg_ref, wu_ref, wd_ref, o_ref = refs
    else:
        x_ref, mod_ref, npre_ref, npost_ref, wg_ref, wu_ref, wd_ref, o_ref = refs
    mod = mod_ref[...]
    npre = npre_ref[...]
    npost = npost_ref[...]
    hid = wg_ref.shape[1]
    sub = min(FFN_SUB_ROWS, x_ref.shape[0])
    blocks = [slice(r, r + sub) for r in range(0, x_ref.shape[0], sub)]

    def prologue(rows):
        x = x_ref[rows, :]
        if has_mix:
            y = jnp.dot(a_ref[rows, :], wo_ref[...], preferred_element_type=F32)
            x = x + mod[5:6] * _rms(y, npost[1:2])
        return x, _modulate(x, npre, mod, s).astype(BF16)

    def swiglu(rows, x, h):
        acc = None
        for c in range(0, hid, FFN_HID_CHUNK):
            g = jnp.dot(h, wg_ref[:, c:c + FFN_HID_CHUNK], preferred_element_type=F32)
            u = jnp.dot(h, wu_ref[:, c:c + FFN_HID_CHUNK], preferred_element_type=F32)
            a = (g * jax.nn.sigmoid(g) * u).astype(BF16)
            part = jnp.dot(a, wd_ref[c:c + FFN_HID_CHUNK, :], preferred_element_type=F32)
            acc = part if acc is None else acc + part
        o_ref[rows, :] = x + FFN_RES_WEIGHT * mod[3 * s + 2:3 * s + 3] * _rms(acc, npost[s:s + 1])

    ready = prologue(blocks[0])
    for k, rows in enumerate(blocks):
        upcoming = prologue(blocks[k + 1]) if k + 1 < len(blocks) else None
        swiglu(rows, *ready)
        ready = upcoming


def _ffn(x2d, rows_per_batch, mod, npre, npost, wg, wu, wd, s, mix=None):
    n, d = x2d.shape
    tm = min(FFN_ROWS, n)
    if mod.shape[0] > 1:
        mod_map = lambda i: ((i * tm) // rows_per_batch, 0, 0)
    else:
        mod_map = lambda i: (0, 0, 0)
    row_spec = pl.BlockSpec((tm, d), lambda i: (i, 0))
    in_specs = [row_spec]
    args = [x2d]
    if mix is not None:
        a2d, wo = mix
        in_specs += [pl.BlockSpec((tm, a2d.shape[1]), lambda i: (i, 0)), _const_spec(wo.shape)]
        args += [a2d, wo]
    in_specs += [pl.BlockSpec((None,) + mod.shape[1:], mod_map), _const_spec(npre.shape),
                 _const_spec(npost.shape), _const_spec(wg.shape), _const_spec(wu.shape),
                 _const_spec(wd.shape)]
    args += [mod, npre, npost, wg, wu, wd]
    return pl.pallas_call(
        functools.partial(_ffn_kernel, s=s, has_mix=mix is not None),
        grid=(n // tm,),
        in_specs=in_specs,
        out_specs=row_spec,
        out_shape=jax.ShapeDtypeStruct((n, d), F32),
        compiler_params=_cparams(("arbitrary",)),
        name="mix_ffn" if mix is not None else "ffn",
    )(*args)


def _qkv_kernel(x_ref, mod_ref, npre_ref, cos_ref, sin_ref, w_ref, q_ref, k_ref, v_ref, *, scale):
    d = x_ref.shape[1]
    half = d // NA_HEADS // 2
    h = _modulate(x_ref[...], npre_ref[...], mod_ref[...], 1).astype(BF16)
    reps = d // cos_ref.shape[1]
    cosf = jnp.tile(cos_ref[...], (1, reps))
    sinf = jnp.tile(sin_ref[...], (1, reps))
    lane = lax.broadcasted_iota(jnp.int32, (x_ref.shape[0], d), 1)
    first = (lane & (2 * half - 1)) < half

    def rope(t):
        partner = jnp.where(first, pltpu.roll(t, d - half, 1), pltpu.roll(t, half, 1))
        return t * cosf + partner * sinf

    q = jnp.dot(h, w_ref[:, 0:d], preferred_element_type=F32)
    q_ref[...] = (rope(q) * scale).astype(BF16)
    k = jnp.dot(h, w_ref[:, d:2 * d], preferred_element_type=F32)
    k_ref[...] = rope(k).astype(BF16)
    v = jnp.dot(h, w_ref[:, 2 * d:3 * d], preferred_element_type=F32)
    v_ref[...] = v.astype(BF16)


def _qkv(x2d, rows_per_batch, mod, npre, cos_t, sin_t, w_qkv):
    n, d = x2d.shape
    tm = min(PROJ_ROWS, rows_per_batch)
    tiles_per_batch = rows_per_batch // tm
    if mod.shape[0] > 1:
        mod_map = lambda i: (i // tiles_per_batch, 0, 0)
    else:
        mod_map = lambda i: (0, 0, 0)
    row_spec = pl.BlockSpec((tm, d), lambda i: (i, 0))
    tab_spec = pl.BlockSpec((tm, cos_t.shape[1]), lambda i: (i % tiles_per_batch, 0))
    out = jax.ShapeDtypeStruct((n, d), BF16)
    return pl.pallas_call(
        functools.partial(_qkv_kernel, scale=float((d // NA_HEADS) ** -0.5) * LOG2_E),
        grid=(n // tm,),
        in_specs=[row_spec, pl.BlockSpec((None,) + mod.shape[1:], mod_map), _const_spec(npre.shape),
                  tab_spec, tab_spec, _const_spec(w_qkv.shape)],
        out_specs=[row_spec, row_spec, row_spec],
        out_shape=[out, out, out],
        compiler_params=_cparams(("arbitrary",)),
        name="qkv_rope",
    )(x2d, mod, npre, cos_t, sin_t, w_qkv)


_NT = (((1,), (1,)), ((), ()))


def _lane_is_head0(shape):
    return lax.broadcasted_iota(jnp.int32, shape, len(shape) - 1) < shape[-1] // 2


def _head_scores(q, hd, kv_blocks, bias):
    is0 = _lane_is_head0(q.shape)
    qh = jnp.where(is0 if hd == 0 else jnp.logical_not(is0), q, jnp.zeros_like(q))
    scores = []
    for blk, (k, _) in enumerate(kv_blocks):
        s = lax.dot_general(qh, k, _NT, preferred_element_type=F32)
        if blk == 0 and bias is not None:
            s = s + bias
        scores.append(s)
    return scores


def _head_output(scores, hd, kv_blocks):
    m = functools.reduce(jnp.maximum, [jnp.max(s, axis=-1, keepdims=True) for s in scores])
    o = None
    for s, (_, v) in zip(scores, kv_blocks):
        is0 = _lane_is_head0(v.shape)
        v_aug = jnp.where(is0 if hd == 0 else jnp.logical_not(is0), v, jnp.ones_like(v))
        part = jnp.dot(jnp.exp2(s - m).astype(BF16), v_aug, preferred_element_type=F32)
        o = part if o is None else o + part
    return o


def _attend(jobs):
    raw = []
    pending = None
    for q, hd, kv_blocks, bias in jobs:
        scores = _head_scores(q, hd, kv_blocks, bias)
        if pending is not None:
            raw.append(_head_output(*pending))
        pending = (scores, hd, kv_blocks)
    raw.append(_head_output(*pending))
    outs = []
    for o0, o1 in zip(raw[0::2], raw[1::2]):
        is0 = _lane_is_head0(o0.shape)
        num = jnp.where(is0, o0, o1)
        den = pltpu.roll(jnp.where(is0, o1, o0), o0.shape[-1] // 2, 1)
        outs.append(num / den)
    return outs


def _attn_kernel(q_ref, k_ref, v_ref, kc_ref, vc_ref, bias_ref, o_ref, *, n_rows):
    nk = bias_ref.shape[-1]
    rb = pl.program_id(2)
    w0 = jnp.clip(rb * ATT_ROWS - NA_WIN_ROWS // 2, 0, n_rows - nk // GRID_W)
    start = pl.multiple_of(w0 * GRID_W, GRID_W)
    jobs = []
    for pp in range(bias_ref.shape[0]):
        lanes = slice(pp * LANES, (pp + 1) * LANES)
        kv_blocks = [(k_ref[pl.ds(start, nk), lanes], v_ref[pl.ds(start, nk), lanes]),
                     (kc_ref[:, lanes], vc_ref[:, lanes])]
        q = q_ref[:, lanes]
        jobs += [(q, hd, kv_blocks, bias_ref[pp, hd]) for hd in range(2)]
    for pp, o in enumerate(_attend(jobs)):
        o_ref[:, pp * LANES:(pp + 1) * LANES] = o.astype(o_ref.dtype)


def _attn_bias_tables(rpb, n_rows):
    heads, n_dr, n_dc = rpb.shape
    win_rows = min(NA_WIN_ROWS, n_rows)
    win_cols = min(NA_WIN_COLS, GRID_W)
    key_rows = ATT_ROWS + NA_WIN_ROWS - 1
    n_blocks = n_rows // ATT_ROWS
    lead = GRID_W - NA_WIN_COLS
    padded = jnp.pad(rpb.astype(F32) * LOG2_E, ((0, 0), (0, 0), (lead, 2 * GRID_W - 1 - lead - n_dc)))
    band = jnp.stack([padded[:, :, GRID_W - 1 - qc:2 * GRID_W - 1 - qc] for qc in range(GRID_W)], axis=2)
    col = np.arange(GRID_W)
    c0 = np.clip(col - win_cols // 2, 0, GRID_W - win_cols)
    valid_c = (col[None, :] >= c0[:, None]) & (col[None, :] < c0[:, None] + win_cols)
    band = jnp.where(jnp.asarray(valid_c)[None, None], band, NEG_BIG)
    masked = jnp.full((heads, GRID_W, GRID_W), NEG_BIG, F32)
    tabs = []
    for blk in (0, 1, n_blocks - 1):
        r_b = blk * ATT_ROWS
        w0 = int(np.clip(r_b - NA_WIN_ROWS // 2, 0, n_rows - key_rows))
        row_slabs = []
        for ri in range(ATT_ROWS):
            r = r_b + ri
            r0 = int(np.clip(r - win_rows // 2, 0, n_rows - win_rows))
            tiles = []
            for kri in range(key_rows):
                kr = w0 + kri
                if r0 <= kr < r0 + win_rows:
                    tiles.append(band[:, kr - r + NA_WIN_ROWS - 1])
                else:
                    tiles.append(masked)
            row_slabs.append(jnp.concatenate(tiles, axis=-1))
        tabs.append(jnp.concatenate(row_slabs, axis=1))
    t = jnp.stack(tabs, axis=0)
    return t.reshape(3, heads // 2, 2, t.shape[-2], t.shape[-1])


def _attention(q, k, v, kc, vc, bias_tabs):
    bsz, n_tok, d = q.shape
    n_ctx = kc.shape[1]
    n_rows = n_tok // GRID_W
    n_blocks = n_rows // ATT_ROWS
    rq = ATT_ROWS * GRID_W
    pair = ATT_PAIRS * LANES
    q_spec = pl.BlockSpec((None, rq, pair), lambda b, hp, rb: (b, rb, hp))
    seq_spec = pl.BlockSpec((None, n_tok, pair), lambda b, hp, rb: (b, 0, hp))
    ctx_spec = pl.BlockSpec((None, n_ctx, pair), lambda b, hp, rb: (b, 0, hp))

    def bias_map(b, hp, rb):
        kind = jnp.where(rb == 0, 0, jnp.where(rb == n_blocks - 1, 2, 1))
        return (kind, hp, 0, 0, 0, 0)

    bias_tabs = bias_tabs.reshape((3, d // pair, ATT_PAIRS) + bias_tabs.shape[2:])
    bias_spec = pl.BlockSpec((None, None) + bias_tabs.shape[2:], bias_map)
    return pl.pallas_call(
        functools.partial(_attn_kernel, n_rows=n_rows),
        grid=(bsz, d // pair, n_blocks),
        in_specs=[q_spec, seq_spec, seq_spec, ctx_spec, ctx_spec, bias_spec],
        out_specs=q_spec,
        out_shape=jax.ShapeDtypeStruct((bsz, n_tok, d), BF16),
        compiler_params=_cparams(("arbitrary", "arbitrary", "arbitrary")),
        name="nbr_attention",
    )(q, k, v, kc, vc, bias_tabs)


def _ctx_attn_kernel(q_ref, k_ref, v_ref, o_ref):
    q = q_ref[...]
    kv_blocks = [(k_ref[...], v_ref[...])]
    (o,) = _attend([(q, hd, kv_blocks, None) for hd in range(2)])
    o_ref[...] = o.astype(o_ref.dtype)


def _ctx_attention(qc, kc, vc):
    bsz, n_ctx, d = qc.shape
    pair = 2 * (d // NA_HEADS)
    spec = pl.BlockSpec((None, n_ctx, pair), lambda b, hp: (b, 0, hp))
    return pl.pallas_call(
        _ctx_attn_kernel,
        grid=(bsz, d // pair),
        in_specs=[spec, spec, spec],
        out_specs=spec,
        out_shape=jax.ShapeDtypeStruct((bsz, n_ctx, d), BF16),
        compiler_params=_cparams(("arbitrary", "arbitrary")),
        name="ctx_attention",
    )(qc, kc, vc)


def _gelu_tanh(x):
    c = float(np.sqrt(2.0 / np.pi))
    return 0.5 * x * (1.0 + jnp.tanh(c * (x + 0.044715 * (x * x * x))))


def _lru_in_kernel(x_ref, xp_ref, xn_ref, mod_ref, npre_ref, wg_ref, wr_ref, cw_ref, cb_ref,
                   u_ref, g_ref, rec_s):
    tm = x_ref.shape[0]
    halo = xp_ref.shape[0]
    i = pl.program_id(1)
    last = pl.num_programs(1) - 1
    mod = mod_ref[...]
    npre = npre_ref[...]
    x_ext = jnp.concatenate([xp_ref[...], x_ref[...], xn_ref[...]], axis=0)
    h_ext = _modulate(x_ext, npre, mod, 1)
    rec = jnp.dot(h_ext.astype(BF16), wr_ref[...], preferred_element_type=F32)
    row = lax.broadcasted_iota(jnp.int32, (tm + 2 * halo, 1), 0)
    valid = ((row >= halo) | (i > 0)) & ((row < halo + tm) | (i < last))
    rec = jnp.where(valid, rec, 0.0)
    cw = cw_ref[...]
    cb = cb_ref[...]
    n_blk, _, blk_w = u_ref.shape
    seg = tm // SUBLANES
    for n in range(n_blk):
        cols = slice(n * blk_w, (n + 1) * blk_w)
        rec_s[n] = rec[:, cols]
        u = cb[:, cols]
        for tap in range(CONV_W):
            u = u + rec_s[n, pl.ds(halo - CONV_PAD_LEFT + tap, tm), :] * cw[tap:tap + 1, cols]
        for j in range(SUBLANES):
            u_ref[n, pl.ds(j, seg, stride=SUBLANES), :] = u[j * seg:(j + 1) * seg]
    hm = h_ext[halo:halo + tm].astype(BF16)
    g_ref[...] = _gelu_tanh(jnp.dot(hm, wg_ref[...], preferred_element_type=F32)).astype(g_ref.dtype)


def _lru_in(x3d, mod, npre, w_gate, w_rec, conv_w, conv_b):
    bsz, n_tok, d = x3d.shape
    width = w_rec.shape[1]
    tm = min(SCAN_ROWS, n_tok)
    n_blk = width // LANES
    halo = SUBLANES
    halo_blocks = n_tok // halo
    per_tile = tm // halo
    if mod.shape[0] > 1:
        mod_map = lambda b, i: (b, 0, 0)
    else:
        mod_map = lambda b, i: (0, 0, 0)
    out_spec = pl.BlockSpec((None, tm, width), lambda b, i: (b, i, 0))
    u_spec = pl.BlockSpec((None, None, n_blk, tm, LANES), lambda b, i: (b, i, 0, 0, 0))
    return pl.pallas_call(
        _lru_in_kernel,
        grid=(bsz, n_tok // tm),
        in_specs=[
            pl.BlockSpec((None, tm, d), lambda b, i: (b, i, 0)),
            pl.BlockSpec((None, halo, d), lambda b, i: (b, jnp.maximum(i * per_tile - 1, 0), 0)),
            pl.BlockSpec((None, halo, d),
                         lambda b, i: (b, jnp.minimum((i + 1) * per_tile, halo_blocks - 1), 0)),
            pl.BlockSpec((None,) + mod.shape[1:], mod_map),
            _const_spec(npre.shape), _const_spec(w_gate.shape), _const_spec(w_rec.shape),
            _const_spec(conv_w.shape), _const_spec((1, width)),
        ],
        out_specs=[u_spec, out_spec],
        out_shape=[jax.ShapeDtypeStruct((bsz, n_tok // tm, n_blk, tm, LANES), F32),
                   jax.ShapeDtypeStruct((bsz, n_tok, width), BF16)],
        scratch_shapes=[pltpu.VMEM((n_blk, tm + 2 * halo, LANES), F32)],
        compiler_params=_cparams(("arbitrary", "arbitrary")),
        name="lru_in",
    )(x3d, x3d, x3d, mod, npre, w_gate, w_rec, conv_w, conv_b.reshape(1, width))


def _scan_kernel(*refs, reverse, combine):
    if combine:
        (u_ref, wcat_ref, ba_ref, bx_ref, lam_ref, init_ref, prev_ref, g_ref,
         o_ref, fin_ref, h_s, p_s, carry_s) = refs
    else:
        (u_ref, wcat_ref, ba_ref, bx_ref, lam_ref, init_ref,
         o_ref, fin_ref, h_s, p_s, carry_s) = refs
    n_blk, tt, blk_w = u_ref.shape
    seg = tt // SUBLANES

    @pl.when(pl.program_id(1) == 0)
    def _():
        carry_s[...] = init_ref[...]

    lam = lam_ref[...]
    softplus = jnp.maximum(-lam, 0.0) + jnp.log1p(jnp.exp(-jnp.abs(lam)))
    half_log2_decay = (-0.5 * LRU_C * LOG2_E) * softplus
    segments = range(SUBLANES - 1, -1, -1) if reverse else range(SUBLANES)

    for n in range(n_blk):
        cols = slice(n * blk_w, (n + 1) * blk_w)
        un = u_ref[n]
        z = jnp.dot(un.astype(BF16), wcat_ref[n], preferred_element_type=F32)
        t_r = jnp.tanh(z[:, :blk_w] + ba_ref[:, cols])
        t_i = jnp.tanh(z[:, blk_w:] + bx_ref[:, cols])
        k = half_log2_decay[:, cols]
        a = jnp.exp2(k + k * t_r)
        one_m = 1.0 - a * a
        p_s[n] = a
        h_s[n] = (one_m * lax.rsqrt(jnp.maximum(one_m, SQRT_TINY))) * ((0.5 + 0.5 * t_i) * un)

    def scan_step(t, carry):
        i = (seg - 1 - t) if reverse else t
        tile = pl.ds(pl.multiple_of(i * SUBLANES, SUBLANES), SUBLANES)
        hs, ps = carry
        new_h, new_p = [], []
        for n in range(n_blk):
            a_t = p_s[n, tile, :]
            h = a_t * hs[n] + h_s[n, tile, :]
            p = a_t * ps[n]
            h_s[n, tile, :] = h
            p_s[n, tile, :] = p
            new_h.append(h)
            new_p.append(p)
        return tuple(new_h), tuple(new_p)

    zeros = tuple(jnp.zeros((SUBLANES, blk_w), F32) for _ in range(n_blk))
    ones = tuple(jnp.ones((SUBLANES, blk_w), F32) for _ in range(n_blk))
    h_end, p_end = lax.fori_loop(0, seg, scan_step, (zeros, ones), unroll=4)

    for n in range(n_blk):
        cols = slice(n * blk_w, (n + 1) * blk_w)
        h, p = h_end[n], p_end[n]
        state = carry_s[:, cols]
        entry = [None] * SUBLANES
        for j in segments:
            entry[j] = state
            state = h[j:j + 1] + p[j:j + 1] * state
        carry_s[:, cols] = state
        for j in range(SUBLANES):
            rows = slice(j * seg, (j + 1) * seg)
            picked = pl.ds(j, seg, stride=SUBLANES)
            ht = h_s[n, picked, :] + p_s[n, picked, :] * entry[j]
            if combine:
                ht = (ht + prev_ref[rows, cols]) * g_ref[rows, cols].astype(F32)
            o_ref[rows, cols] = ht.astype(o_ref.dtype)

    fin_ref[...] = carry_s[...]


def _scan(u, width, wcat, b_a, b_x, lam, init, reverse, prev=None, gate=None):
    bsz, n_chunks, n_blk, tt, blk_w = u.shape
    n_tok = n_chunks * tt
    combine = prev is not None
    if reverse:
        chunk = lambda i: n_chunks - 1 - i
    else:
        chunk = lambda i: i
    u_spec = pl.BlockSpec((None, None, n_blk, tt, blk_w), lambda b, i: (b, chunk(i), 0, 0, 0))
    seq_spec = pl.BlockSpec((None, tt, width), lambda b, i: (b, chunk(i), 0))
    vec_spec = pl.BlockSpec((None, 1, width), lambda b, i: (b, 0, 0))
    in_specs = [u_spec, _const_spec(wcat.shape), _const_spec((1, width)), _const_spec((1, width)),
                _const_spec((1, width)), vec_spec]
    args = [u, wcat, b_a.reshape(1, width), b_x.reshape(1, width), lam.reshape(1, width), init]
    if combine:
        in_specs += [seq_spec, seq_spec]
        args += [prev, gate]
    return pl.pallas_call(
        functools.partial(_scan_kernel, reverse=reverse, combine=combine),
        grid=(bsz, n_chunks),
        in_specs=in_specs,
        out_specs=[seq_spec, vec_spec],
        out_shape=[jax.ShapeDtypeStruct((bsz, n_tok, width), BF16 if combine else F32),
                   jax.ShapeDtypeStruct((bsz, 1, width), F32)],
        scratch_shapes=[pltpu.VMEM((n_blk, tt, blk_w), F32), pltpu.VMEM((n_blk, tt, blk_w), F32),
                        pltpu.VMEM((1, width), F32)],
        compiler_params=_cparams(("arbitrary", "arbitrary")),
        name="lru_scan_bwd" if reverse else "lru_scan_fwd",
    )(*args)


def _bidirectional_lru(u, gate, wcat, b_a, b_x, lam, init_f, init_b):
    width = gate.shape[-1]
    h_f, fin_f = _scan(u, width, wcat[0], b_a[0], b_x[0], lam[0], init_f, reverse=False)
    act, fin_b = _scan(u, width, wcat[1], b_a[1], b_x[1], lam[1], init_b, reverse=True,
                       prev=h_f, gate=gate)
    return act, fin_f, fin_b


def _rope_tables(n_tok, head_dim):
    pairs = head_dim // 4
    t = jnp.arange(n_tok, dtype=jnp.int32)
    row = (t // GRID_W).astype(F32)
    col = (t % GRID_W).astype(F32)
    inv_freq = jnp.power(ROPE_BASE, -jnp.arange(pairs, dtype=F32) / pairs)
    ang = jnp.concatenate([row[:, None] * inv_freq, col[:, None] * inv_freq], axis=-1)
    cos, sin = jnp.cos(ang), jnp.sin(ang)
    cos_h = jnp.concatenate([cos, cos], axis=-1)
    sin_h = jnp.concatenate([-sin, sin], axis=-1)
    return jnp.tile(cos_h, (1, 2)), jnp.tile(sin_h, (1, 2))


def kernel(x, c, ctx, c_ctx, ada_w, ada_b, norm_pre, norm_post, ffn_w_gate, ffn_w_up, ffn_w_down,
           na_w_qkv, na_w_o, na_rpb, lru_w_in, lru_conv_w, lru_conv_b, lru_w_a, lru_b_a, lru_w_x,
           lru_b_x, lru_lambda, lru_w_o):
    bsz, n_tok, d = x.shape
    n_ctx = ctx.shape[1]
    depth = ada_w.shape[0]
    n_mod = ada_w.shape[2] // d
    head_dim = d // NA_HEADS
    lru_width = lru_w_o.shape[1]
    n_mixers = 2

    cond = jnp.concatenate([c, c_ctx[None, :], jnp.zeros((SUBLANES - bsz - 1, d), F32)], axis=0)
    mods = _adaln(cond, ada_w, ada_b).reshape(depth, SUBLANES, n_mod, d)

    cos_t, sin_t = _rope_tables(n_tok, head_dim)
    ones_t = jnp.ones((n_ctx, cos_t.shape[1]), F32)
    zeros_t = jnp.zeros((n_ctx, cos_t.shape[1]), F32)

    wg_all = ffn_w_gate.astype(BF16)
    wu_all = ffn_w_up.astype(BF16)
    wd_all = ffn_w_down.astype(BF16)

    xl = x.reshape(bsz * n_tok, d)
    xc = ctx.reshape(bsz * n_ctx, d)
    for i in range(depth):
        last = i == depth - 1
        j = i // n_mixers
        mod = mods[i, :bsz]
        mod_c = mods[i, bsz:bsz + 1]
        npre, npost = norm_pre[i], norm_post[i]

        def half_ffn(u2d, rows, m, s, mix=None):
            f = s // 2
            return _ffn(u2d, rows, m, npre, npost, wg_all[i, f], wu_all[i, f], wd_all[i, f], s, mix)

        xl = half_ffn(xl, n_tok, mod, 0)
        xc = half_ffn(xc, n_ctx, mod_c, 0)
        if i % n_mixers == 0:
            w_qkv = na_w_qkv[j].astype(BF16)
            w_o = na_w_o[j].astype(BF16)
            q, k, v = _qkv(xl, n_tok, mod, npre, cos_t, sin_t, w_qkv)
            qc, kc, vc = _qkv(xc, n_ctx, mod_c, npre, ones_t, zeros_t, w_qkv)
            shp, shp_c = (bsz, n_tok, d), (bsz, n_ctx, d)
            kc3, vc3 = kc.reshape(shp_c), vc.reshape(shp_c)
            bias_tabs = _attn_bias_tables(na_rpb[j], n_tok // GRID_W)
            act = _attention(q.reshape(shp), k.reshape(shp), v.reshape(shp), kc3, vc3, bias_tabs)
            act = act.reshape(bsz * n_tok, d)
            if not last:
                act_c = _ctx_attention(qc.reshape(shp_c), kc3, vc3).reshape(bsz * n_ctx, d)
        else:
            w_in = lru_w_in[j].astype(BF16)
            w_gate, w_rec = w_in[:, :lru_width], w_in[:, lru_width:]
            w_o = lru_w_o[j].astype(BF16)
            wcat = (0.5 * jnp.concatenate([lru_w_a[j], lru_w_x[j]], axis=-1)).astype(BF16)
            half_b_a, half_b_x = 0.5 * lru_b_a[j], 0.5 * lru_b_x[j]
            u_l, g_l = _lru_in(xl.reshape(bsz, n_tok, d), mod, npre, w_gate, w_rec,
                               lru_conv_w[j], lru_conv_b[j])
            u_c, g_c = _lru_in(xc.reshape(bsz, n_ctx, d), mod_c, npre, w_gate, w_rec,
                               lru_conv_w[j], lru_conv_b[j])
            zero_state = jnp.zeros((bsz, 1, lru_width), F32)
            act_c, fin_f, fin_b = _bidirectional_lru(u_c, g_c, wcat, half_b_a, half_b_x,
                                                     lru_lambda[j], zero_state, zero_state)
            act, _, _ = _bidirectional_lru(u_l, g_l, wcat, half_b_a, half_b_x,
                                           lru_lambda[j], fin_f, fin_b)
            act = act.reshape(bsz * n_tok, lru_width)
            act_c = act_c.reshape(bsz * n_ctx, lru_width)
        xl = half_ffn(xl, n_tok, mod, 2, mix=(act, w_o))
        if not last:
            xc = half_ffn(xc, n_ctx, mod_c, 2, mix=(act_c, w_o))
    return xl.reshape(bsz, n_tok, d)
```

```python
import functools

import numpy as np
import jax
import jax.numpy as jnp
from jax import lax
from jax.experimental import pallas as pl
from jax.experimental.pallas import tpu as pltpu

F32 = jnp.float32
BF16 = jnp.bfloat16

GRID_W = 64
NA_HEADS = 16
NA_WIN_ROWS = 8
NA_WIN_COLS = 16
ROPE_BASE = 10000.0
LRU_C = 8.0
CONV_W = 4
CONV_PAD_LEFT = 2
RMS_EPS = 1e-6
FFN_RES_WEIGHT = 0.5

LANES = 128
SUBLANES = 8
MXU_DIM = 256

FFN_ROWS = 1024
FFN_SUB_ROWS = 512
FFN_HID_CHUNK = MXU_DIM
PROJ_ROWS = 512
ATT_ROWS = 4
ATT_PAIRS = 2
SCAN_ROWS = 512
MOD_COLS = 1024
VMEM_LIMIT = 56 * 1024 * 1024
NEG_BIG = -1e30
SQRT_TINY = 1e-30
LOG2_E = float(np.log2(np.e))


def _cparams(sem):
    return pltpu.CompilerParams(dimension_semantics=sem, vmem_limit_bytes=VMEM_LIMIT)


def _const_spec(shape):
    nd = len(shape)
    return pl.BlockSpec(shape, lambda *_: (0,) * nd, pipeline_mode=pl.Buffered(1))


def _layer_spec(arr, lead, cols=None):
    tail = arr.shape[len(lead):]
    if cols is not None:
        tail = tail[:-1] + (cols[0],)
    idx = tuple(lead) + (0,) * (len(tail) - 1) + (0 if cols is None else cols[1],)
    return pl.BlockSpec((None,) * len(lead) + tail, lambda *_: idx, pipeline_mode=pl.Buffered(1))


def _mod_spec(mod, batch_of):
    table, layer, row, per_batch = mod
    if per_batch:
        index_map = lambda *g: (layer, row + batch_of(*g), 0, 0)
    else:
        index_map = lambda *g: (layer, row, 0, 0)
    return pl.BlockSpec((None, None) + table.shape[2:], index_map)


def _rms(x, gain):
    ms = jnp.mean(x * x, axis=-1, keepdims=True)
    return x * lax.rsqrt(ms + RMS_EPS) * gain


def _modulate(x, npre, mod, s):
    return _rms(x, npre[s:s + 1]) * (1.0 + mod[3 * s + 1:3 * s + 2]) + mod[3 * s:3 * s + 1]


def _mod_kernel(c_ref, w_ref, b_ref, o_ref):
    c = c_ref[...]
    s = (c * jax.nn.sigmoid(c)).astype(BF16)
    o_ref[...] = jnp.dot(s, w_ref[...].astype(BF16), preferred_element_type=F32) + b_ref[...]


def _adaln(cond, ada_w, ada_b):
    depth, d, n = ada_w.shape
    rows = cond.shape[0]
    return pl.pallas_call(
        _mod_kernel,
        grid=(depth, n // MOD_COLS),
        in_specs=[
            pl.BlockSpec((rows, d), lambda l, j: (0, 0)),
            pl.BlockSpec((None, d, MOD_COLS), lambda l, j: (l, 0, j)),
            pl.BlockSpec((None, 1, MOD_COLS), lambda l, j: (l, 0, j)),
        ],
        out_specs=pl.BlockSpec((None, rows, MOD_COLS), lambda l, j: (l, 0, j)),
        out_shape=jax.ShapeDtypeStruct((depth, rows, n), F32),
        compiler_params=_cparams(("arbitrary", "arbitrary")),
        name="adaln_mod",
    )(cond, ada_w, ada_b.reshape(depth, 1, n))


def _ffn_kernel(*refs, s, has_mix):
    if has_mix:
        x_ref, a_ref, wo_ref, mod_ref, npre_ref, npost_ref, wg_ref, wu_ref, wd_ref, o_ref = refs
    else:
        x_ref, mod_ref, npre_ref, npost_ref, wg_ref, wu_ref, wd_ref, o_ref = refs
    mod = mod_ref[...]
    npre = npre_ref[...]
    npost = npost_ref[...]
    hid = wg_ref.shape[1]
    sub = min(FFN_SUB_ROWS, x_ref.shape[0])
    blocks = [slice(r, r + sub) for r in range(0, x_ref.shape[0], sub)]

    def prologue(rows):
        x = x_ref[rows, :]
        if has_mix:
            y = jnp.dot(a_ref[rows, :], wo_ref[...], preferred_element_type=F32)
            x = x + mod[5:6] * _rms(y, npost[1:2])
        return x, _modulate(x, npre, mod, s).astype(BF16)

    def swiglu(rows, x, h):
        acc = None
        for c in range(0, hid, FFN_HID_CHUNK):
            g = jnp.dot(h, wg_ref[:, c:c + FFN_HID_CHUNK], preferred_element_type=F32)
            u = jnp.dot(h, wu_ref[:, c:c + FFN_HID_CHUNK], preferred_element_type=F32)
            a = (g * jax.nn.sigmoid(g) * u).astype(BF16)
            part = jnp.dot(a, wd_ref[c:c + FFN_HID_CHUNK, :], preferred_element_type=F32)
            acc = part if acc is None else acc + part
        o_ref[rows, :] = x + FFN_RES_WEIGHT * mod[3 * s + 2:3 * s + 3] * _rms(acc, npost[s:s + 1])

    ready = prologue(blocks[0])
    for k, rows in enumerate(blocks):
        upcoming = prologue(blocks[k + 1]) if k + 1 < len(blocks) else None
        swiglu(rows, *ready)
        ready = upcoming


def _ffn(x2d, rows_per_batch, mod, layer, norm_pre, norm_post, wg, wu, wd, s, mix=None):
    n, d = x2d.shape
    tm = min(FFN_ROWS, n)
    half = (layer, s // 2)
    row_spec = pl.BlockSpec((tm, d), lambda i: (i, 0))
    in_specs = [row_spec]
    args = [x2d]
    if mix is not None:
        a2d, wo, wo_layer = mix
        in_specs += [pl.BlockSpec((tm, a2d.shape[1]), lambda i: (i, 0)), _layer_spec(wo, (wo_layer,))]
        args += [a2d, wo]
    in_specs += [_mod_spec(mod, lambda i: (i * tm) // rows_per_batch),
                 _layer_spec(norm_pre, (layer,)), _layer_spec(norm_post, (layer,)),
                 _layer_spec(wg, half), _layer_spec(wu, half), _layer_spec(wd, half)]
    args += [mod[0], norm_pre, norm_post, wg, wu, wd]
    return pl.pallas_call(
        functools.partial(_ffn_kernel, s=s, has_mix=mix is not None),
        grid=(n // tm,),
        in_specs=in_specs,
        out_specs=row_spec,
        out_shape=jax.ShapeDtypeStruct((n, d), F32),
        compiler_params=_cparams(("arbitrary",)),
        name="mix_ffn" if mix is not None else "ffn",
    )(*args)


def _qkv_kernel(x_ref, mod_ref, npre_ref, cos_ref, sin_ref, w_ref, q_ref, k_ref, v_ref, *, scale):
    d = x_ref.shape[1]
    half = d // NA_HEADS // 2
    h = _modulate(x_ref[...], npre_ref[...], mod_ref[...], 1).astype(BF16)
    reps = d // cos_ref.shape[1]
    cosf = jnp.tile(cos_ref[...], (1, reps))
    sinf = jnp.tile(sin_ref[...], (1, reps))
    lane = lax.broadcasted_iota(jnp.int32, (x_ref.shape[0], d), 1)
    first = (lane & (2 * half - 1)) < half

    def rope(t):
        partner = jnp.where(first, pltpu.roll(t, d - half, 1), pltpu.roll(t, half, 1))
        return t * cosf + partner * sinf

    q = jnp.dot(h, w_ref[:, 0:d], preferred_element_type=F32)
    q_ref[...] = (rope(q) * scale).astype(BF16)
    k = jnp.dot(h, w_ref[:, d:2 * d], preferred_element_type=F32)
    k_ref[...] = rope(k).astype(BF16)
    v = jnp.dot(h, w_ref[:, 2 * d:3 * d], preferred_element_type=F32)
    v_ref[...] = v.astype(BF16)


def _qkv(x2d, rows_per_batch, mod, layer, norm_pre, cos_t, sin_t, w_qkv, w_layer):
    n, d = x2d.shape
    tm = min(PROJ_ROWS, rows_per_batch)
    tiles_per_batch = rows_per_batch // tm
    row_spec = pl.BlockSpec((tm, d), lambda i: (i, 0))
    tab_spec = pl.BlockSpec((tm, cos_t.shape[1]), lambda i: (i % tiles_per_batch, 0))
    out = jax.ShapeDtypeStruct((n, d), BF16)
    return pl.pallas_call(
        functools.partial(_qkv_kernel, scale=float((d // NA_HEADS) ** -0.5) * LOG2_E),
        grid=(n // tm,),
        in_specs=[row_spec, _mod_spec(mod, lambda i: i // tiles_per_batch),
                  _layer_spec(norm_pre, (layer,)), tab_spec, tab_spec, _layer_spec(w_qkv, (w_layer,))],
        out_specs=[row_spec, row_spec, row_spec],
        out_shape=[out, out, out],
        compiler_params=_cparams(("arbitrary",)),
        name="qkv_rope",
    )(x2d, mod[0], norm_pre, cos_t, sin_t, w_qkv)


_NT = (((1,), (1,)), ((), ()))


def _lane_is_head0(shape):
    return lax.broadcasted_iota(jnp.int32, shape, len(shape) - 1) < shape[-1] // 2


def _head_scores(q, hd, kv_blocks, bias):
    is0 = _lane_is_head0(q.shape)
    qh = jnp.where(is0 if hd == 0 else jnp.logical_not(is0), q, jnp.zeros_like(q))
    scores = []
    for blk, (k, _) in enumerate(kv_blocks):
        s = lax.dot_general(qh, k, _NT, preferred_element_type=F32)
        if blk == 0 and bias is not None:
            s = s + bias
        scores.append(s)
    return scores


def _head_output(scores, hd, kv_blocks):
    m = functools.reduce(jnp.maximum, [jnp.max(s, axis=-1, keepdims=True) for s in scores])
    o = None
    for s, (_, v) in zip(scores, kv_blocks):
        is0 = _lane_is_head0(v.shape)
        v_aug = jnp.where(is0 if hd == 0 else jnp.logical_not(is0), v, jnp.ones_like(v))
        part = jnp.dot(jnp.exp2(s - m).astype(BF16), v_aug, preferred_element_type=F32)
        o = part if o is None else o + part
    return o


def _attend(jobs):
    raw = []
    pending = None
    for q, hd, kv_blocks, bias in jobs:
        scores = _head_scores(q, hd, kv_blocks, bias)
        if pending is not None:
            raw.append(_head_output(*pending))
        pending = (scores, hd, kv_blocks)
    raw.append(_head_output(*pending))
    outs = []
    for o0, o1 in zip(raw[0::2], raw[1::2]):
        is0 = _lane_is_head0(o0.shape)
        num = jnp.where(is0, o0, o1)
        den = pltpu.roll(jnp.where(is0, o1, o0), o0.shape[-1] // 2, 1)
        outs.append(num / den)
    return outs


def _attn_kernel(q_ref, k_ref, v_ref, kc_ref, vc_ref, bias_ref, o_ref, *, n_rows):
    nk = bias_ref.shape[-1]
    rb = pl.program_id(2)
    w0 = jnp.clip(rb * ATT_ROWS - NA_WIN_ROWS // 2, 0, n_rows - nk // GRID_W)
    start = pl.multiple_of(w0 * GRID_W, GRID_W)
    jobs = []
    for pp in range(bias_ref.shape[0]):
        lanes = slice(pp * LANES, (pp + 1) * LANES)
        kv_blocks = [(k_ref[pl.ds(start, nk), lanes], v_ref[pl.ds(start, nk), lanes]),
                     (kc_ref[:, lanes], vc_ref[:, lanes])]
        q = q_ref[:, lanes]
        jobs += [(q, hd, kv_blocks, bias_ref[pp, hd]) for hd in range(2)]
    for pp, o in enumerate(_attend(jobs)):
        o_ref[:, pp * LANES:(pp + 1) * LANES] = o.astype(o_ref.dtype)


def _attn_bias_tables(rpb, n_rows):
    heads, n_dr, n_dc = rpb.shape
    win_rows = min(NA_WIN_ROWS, n_rows)
    win_cols = min(NA_WIN_COLS, GRID_W)
    key_rows = ATT_ROWS + NA_WIN_ROWS - 1
    n_blocks = n_rows // ATT_ROWS
    lead = GRID_W - NA_WIN_COLS
    padded = jnp.pad(rpb.astype(F32) * LOG2_E, ((0, 0), (0, 0), (lead, 2 * GRID_W - 1 - lead - n_dc)))
    band = jnp.stack([padded[:, :, GRID_W - 1 - qc:2 * GRID_W - 1 - qc] for qc in range(GRID_W)], axis=2)
    col = np.arange(GRID_W)
    c0 = np.clip(col - win_cols // 2, 0, GRID_W - win_cols)
    valid_c = (col[None, :] >= c0[:, None]) & (col[None, :] < c0[:, None] + win_cols)
    band = jnp.where(jnp.asarray(valid_c)[None, None], band, NEG_BIG)
    masked = jnp.full((heads, 1, GRID_W, GRID_W), NEG_BIG, F32)
    band = jnp.concatenate([band, masked], axis=1)
    pick = np.zeros((3, ATT_ROWS, key_rows, n_dr + 1), np.float32)
    for kind, blk in enumerate((0, 1, n_blocks - 1)):
        r_b = blk * ATT_ROWS
        w0 = int(np.clip(r_b - NA_WIN_ROWS // 2, 0, n_rows - key_rows))
        for ri in range(ATT_ROWS):
            r = r_b + ri
            r0 = int(np.clip(r - win_rows // 2, 0, n_rows - win_rows))
            for kri in range(key_rows):
                kr = w0 + kri
                valid = r0 <= kr < r0 + win_rows
                pick[kind, ri, kri, kr - r + NA_WIN_ROWS - 1 if valid else n_dr] = 1.0
    t = jnp.einsum('trke,heqc->thrqkc', jnp.asarray(pick), band, precision=lax.Precision.HIGHEST)
    return t.reshape(3, heads // 2, 2, ATT_ROWS * GRID_W, key_rows * GRID_W)


def _attention(q, k, v, kc, vc, bias_tabs):
    bsz, n_tok, d = q.shape
    n_ctx = kc.shape[1]
    n_rows = n_tok // GRID_W
    n_blocks = n_rows // ATT_ROWS
    rq = ATT_ROWS * GRID_W
    pair = ATT_PAIRS * LANES
    q_spec = pl.BlockSpec((None, rq, pair), lambda b, hp, rb: (b, rb, hp))
    seq_spec = pl.BlockSpec((None, n_tok, pair), lambda b, hp, rb: (b, 0, hp))
    ctx_spec = pl.BlockSpec((None, n_ctx, pair), lambda b, hp, rb: (b, 0, hp))

    def bias_map(b, hp, rb):
        kind = jnp.where(rb == 0, 0, jnp.where(rb == n_blocks - 1, 2, 1))
        return (kind, hp, 0, 0, 0, 0)

    bias_tabs = bias_tabs.reshape((3, d // pair, ATT_PAIRS) + bias_tabs.shape[2:])
    bias_spec = pl.BlockSpec((None, None) + bias_tabs.shape[2:], bias_map)
    return pl.pallas_call(
        functools.partial(_attn_kernel, n_rows=n_rows),
        grid=(bsz, d // pair, n_blocks),
        in_specs=[q_spec, seq_spec, seq_spec, ctx_spec, ctx_spec, bias_spec],
        out_specs=q_spec,
        out_shape=jax.ShapeDtypeStruct((bsz, n_tok, d), BF16),
        compiler_params=_cparams(("arbitrary", "arbitrary", "arbitrary")),
        name="nbr_attention",
    )(q, k, v, kc, vc, bias_tabs)


def _ctx_attn_kernel(q_ref, k_ref, v_ref, o_ref):
    q = q_ref[...]
    kv_blocks = [(k_ref[...], v_ref[...])]
    (o,) = _attend([(q, hd, kv_blocks, None) for hd in range(2)])
    o_ref[...] = o.astype(o_ref.dtype)


def _ctx_attention(qc, kc, vc):
    bsz, n_ctx, d = qc.shape
    pair = 2 * (d // NA_HEADS)
    spec = pl.BlockSpec((None, n_ctx, pair), lambda b, hp: (b, 0, hp))
    return pl.pallas_call(
        _ctx_attn_kernel,
        grid=(bsz, d // pair),
        in_specs=[spec, spec, spec],
        out_specs=spec,
        out_shape=jax.ShapeDtypeStruct((bsz, n_ctx, d), BF16),
        compiler_params=_cparams(("arbitrary", "arbitrary")),
        name="ctx_attention",
    )(qc, kc, vc)


def _gelu_tanh(x):
    c = float(np.sqrt(2.0 / np.pi))
    return 0.5 * x * (1.0 + jnp.tanh(c * (x + 0.044715 * (x * x * x))))


def _lru_in_kernel(x_ref, xp_ref, xn_ref, mod_ref, npre_ref, wg_ref, wr_ref, cw_ref, cb_ref,
                   u_ref, g_ref, rec_s):
    tm = x_ref.shape[0]
    halo = xp_ref.shape[0]
    i = pl.program_id(1)
    last = pl.num_programs(1) - 1
    mod = mod_ref[...]
    npre = npre_ref[...]
    x_ext = jnp.concatenate([xp_ref[...], x_ref[...], xn_ref[...]], axis=0)
    h_ext = _modulate(x_ext, npre, mod, 1)
    rec = jnp.dot(h_ext.astype(BF16), wr_ref[...], preferred_element_type=F32)
    gate = jnp.dot(h_ext[halo:halo + tm].astype(BF16), wg_ref[...], preferred_element_type=F32)
    row = lax.broadcasted_iota(jnp.int32, (tm + 2 * halo, 1), 0)
    valid = ((row >= halo) | (i > 0)) & ((row < halo + tm) | (i < last))
    rec = jnp.where(valid, rec, 0.0)
    cw = cw_ref[...]
    cb = cb_ref[...]
    n_blk, _, blk_w = u_ref.shape
    seg = tm // SUBLANES
    for n in range(n_blk):
        cols = slice(n * blk_w, (n + 1) * blk_w)
        rec_s[n] = rec[:, cols]
        u = cb[:, cols]
        for tap in range(CONV_W):
            u = u + rec_s[n, pl.ds(halo - CONV_PAD_LEFT + tap, tm), :] * cw[tap:tap + 1, cols]
        for j in range(SUBLANES):
            u_ref[n, pl.ds(j, seg, stride=SUBLANES), :] = u[j * seg:(j + 1) * seg]
    g_ref[...] = _gelu_tanh(gate).astype(g_ref.dtype)


def _lru_in(x3d, mod, layer, norm_pre, w_in, conv_w, conv_b, w_layer):
    bsz, n_tok, d = x3d.shape
    width = w_in.shape[2] // 2
    tm = min(SCAN_ROWS, n_tok)
    n_blk = width // LANES
    halo = SUBLANES
    halo_blocks = n_tok // halo
    per_tile = tm // halo
    out_spec = pl.BlockSpec((None, tm, width), lambda b, i: (b, i, 0))
    u_spec = pl.BlockSpec((None, None, n_blk, tm, LANES), lambda b, i: (b, i, 0, 0, 0))
    return pl.pallas_call(
        _lru_in_kernel,
        grid=(bsz, n_tok // tm),
        in_specs=[
            pl.BlockSpec((None, tm, d), lambda b, i: (b, i, 0)),
            pl.BlockSpec((None, halo, d), lambda b, i: (b, jnp.maximum(i * per_tile - 1, 0), 0)),
            pl.BlockSpec((None, halo, d),
                         lambda b, i: (b, jnp.minimum((i + 1) * per_tile, halo_blocks - 1), 0)),
            _mod_spec(mod, lambda b, i: b),
            _layer_spec(norm_pre, (layer,)),
            _layer_spec(w_in, (w_layer,), cols=(width, 0)), _layer_spec(w_in, (w_layer,), cols=(width, 1)),
            _layer_spec(conv_w, (w_layer,)), _layer_spec(conv_b, (w_layer,)),
        ],
        out_specs=[u_spec, out_spec],
        out_shape=[jax.ShapeDtypeStruct((bsz, n_tok // tm, n_blk, tm, LANES), F32),
                   jax.ShapeDtypeStruct((bsz, n_tok, width), BF16)],
        scratch_shapes=[pltpu.VMEM((n_blk, tm + 2 * halo, LANES), F32)],
        compiler_params=_cparams(("arbitrary", "arbitrary")),
        name="lru_in",
    )(x3d, x3d, x3d, mod[0], norm_pre, w_in, w_in, conv_w, conv_b)


def _scan_kernel(*refs, reverse, combine):
    if combine:
        (u_ref, wcat_ref, ba_ref, bx_ref, lam_ref, init_ref, prev_ref, g_ref,
         o_ref, fin_ref, h_s, p_s, carry_s) = refs
    else:
        (u_ref, wcat_ref, ba_ref, bx_ref, lam_ref, init_ref,
         o_ref, fin_ref, h_s, p_s, carry_s) = refs
    n_blk, tt, blk_w = u_ref.shape
    seg = tt // SUBLANES

    @pl.when(pl.program_id(1) == 0)
    def _():
        carry_s[...] = init_ref[...]

    lam = lam_ref[...]
    softplus = jnp.maximum(-lam, 0.0) + jnp.log1p(jnp.exp(-jnp.abs(lam)))
    half_log2_decay = (-0.5 * LRU_C * LOG2_E) * softplus
    segments = range(SUBLANES - 1, -1, -1) if reverse else range(SUBLANES)

    for n in range(n_blk):
        cols = slice(n * blk_w, (n + 1) * blk_w)
        un = u_ref[n]
        z = jnp.dot(un.astype(BF16), wcat_ref[n], preferred_element_type=F32)
        t_r = jnp.tanh(z[:, :blk_w] + ba_ref[:, cols])
        t_i = jnp.tanh(z[:, blk_w:] + bx_ref[:, cols])
        k = half_log2_decay[:, cols]
        a = jnp.exp2(k + k * t_r)
        one_m = 1.0 - a * a
        p_s[n] = a
        h_s[n] = (one_m * lax.rsqrt(jnp.maximum(one_m, SQRT_TINY))) * ((0.5 + 0.5 * t_i) * un)

    def scan_step(t, carry):
        i = (seg - 1 - t) if reverse else t
        tile = pl.ds(pl.multiple_of(i * SUBLANES, SUBLANES), SUBLANES)
        hs, ps = carry
        new_h, new_p = [], []
        for n in range(n_blk):
            a_t = p_s[n, tile, :]
            h = a_t * hs[n] + h_s[n, tile, :]
            p = a_t * ps[n]
            h_s[n, tile, :] = h
            p_s[n, tile, :] = p
            new_h.append(h)
            new_p.append(p)
        return tuple(new_h), tuple(new_p)

    zeros = tuple(jnp.zeros((SUBLANES, blk_w), F32) for _ in range(n_blk))
    ones = tuple(jnp.ones((SUBLANES, blk_w), F32) for _ in range(n_blk))
    h_end, p_end = lax.fori_loop(0, seg, scan_step, (zeros, ones), unroll=4)

    for n in range(n_blk):
        cols = slice(n * blk_w, (n + 1) * blk_w)
        h, p = h_end[n], p_end[n]
        state = carry_s[:, cols]
        entry = [None] * SUBLANES
        for j in segments:
            entry[j] = state
            state = h[j:j + 1] + p[j:j + 1] * state
        carry_s[:, cols] = state
        for j in range(SUBLANES):
            rows = slice(j * seg, (j + 1) * seg)
            picked = pl.ds(j, seg, stride=SUBLANES)
            ht = h_s[n, picked, :] + p_s[n, picked, :] * entry[j]
            if combine:
                ht = (ht + prev_ref[rows, cols]) * g_ref[rows, cols].astype(F32)
            o_ref[rows, cols] = ht.astype(o_ref.dtype)

    fin_ref[...] = carry_s[...]


def _scan(u, width, wcat, b_a, b_x, lam, init, reverse, prev=None, gate=None):
    bsz, n_chunks, n_blk, tt, blk_w = u.shape
    n_tok = n_chunks * tt
    combine = prev is not None
    if reverse:
        chunk = lambda i: n_chunks - 1 - i
    else:
        chunk = lambda i: i
    u_spec = pl.BlockSpec((None, None, n_blk, tt, blk_w), lambda b, i: (b, chunk(i), 0, 0, 0))
    seq_spec = pl.BlockSpec((None, tt, width), lambda b, i: (b, chunk(i), 0))
    vec_spec = pl.BlockSpec((None, 1, width), lambda b, i: (b, 0, 0))
    in_specs = [u_spec, _const_spec(wcat.shape), _const_spec((1, width)), _const_spec((1, width)),
                _const_spec((1, width)), vec_spec]
    args = [u, wcat, b_a.reshape(1, width), b_x.reshape(1, width), lam.reshape(1, width), init]
    if combine:
        in_specs += [seq_spec, seq_spec]
        args += [prev, gate]
    return pl.pallas_call(
        functools.partial(_scan_kernel, reverse=reverse, combine=combine),
        grid=(bsz, n_chunks),
        in_specs=in_specs,
        out_specs=[seq_spec, vec_spec],
        out_shape=[jax.ShapeDtypeStruct((bsz, n_tok, width), BF16 if combine else F32),
                   jax.ShapeDtypeStruct((bsz, 1, width), F32)],
        scratch_shapes=[pltpu.VMEM((n_blk, tt, blk_w), F32), pltpu.VMEM((n_blk, tt, blk_w), F32),
                        pltpu.VMEM((1, width), F32)],
        compiler_params=_cparams(("arbitrary", "arbitrary")),
        name="lru_scan_bwd" if reverse else "lru_scan_fwd",
    )(*args)


def _bidirectional_lru(u, gate, wcat, b_a, b_x, lam, init_f, init_b):
    width = gate.shape[-1]
    h_f, fin_f = _scan(u, width, wcat[0], b_a[0], b_x[0], lam[0], init_f, reverse=False)
    act, fin_b = _scan(u, width, wcat[1], b_a[1], b_x[1], lam[1], init_b, reverse=True,
                       prev=h_f, gate=gate)
    return act, fin_f, fin_b


def _rope_tables(n_tok, head_dim):
    pairs = head_dim // 4
    t = jnp.arange(n_tok, dtype=jnp.int32)
    row = (t // GRID_W).astype(F32)
    col = (t % GRID_W).astype(F32)
    inv_freq = jnp.power(ROPE_BASE, -jnp.arange(pairs, dtype=F32) / pairs)
    ang = jnp.concatenate([row[:, None] * inv_freq, col[:, None] * inv_freq], axis=-1)
    cos, sin = jnp.cos(ang), jnp.sin(ang)
    cos_h = jnp.concatenate([cos, cos], axis=-1)
    sin_h = jnp.concatenate([-sin, sin], axis=-1)
    return jnp.tile(cos_h, (1, 2)), jnp.tile(sin_h, (1, 2))


def kernel(x, c, ctx, c_ctx, ada_w, ada_b, norm_pre, norm_post, ffn_w_gate, ffn_w_up, ffn_w_down,
           na_w_qkv, na_w_o, na_rpb, lru_w_in, lru_conv_w, lru_conv_b, lru_w_a, lru_b_a, lru_w_x,
           lru_b_x, lru_lambda, lru_w_o):
    bsz, n_tok, d = x.shape
    n_ctx = ctx.shape[1]
    depth = ada_w.shape[0]
    n_mod = ada_w.shape[2] // d
    head_dim = d // NA_HEADS
    lru_width = lru_w_o.shape[1]
    n_mixers = 2

    cond = jnp.concatenate([c, c_ctx[None, :], jnp.zeros((SUBLANES - bsz - 1, d), F32)], axis=0)
    mods = _adaln(cond, ada_w, ada_b).reshape(depth, SUBLANES, n_mod, d)

    cos_t, sin_t = _rope_tables(n_tok, head_dim)
    ones_t = jnp.ones((n_ctx, cos_t.shape[1]), F32)
    zeros_t = jnp.zeros((n_ctx, cos_t.shape[1]), F32)

    wg_all, wu_all, wd_all = (w.astype(BF16) for w in (ffn_w_gate, ffn_w_up, ffn_w_down))
    w_qkv_all, na_wo_all = na_w_qkv.astype(BF16), na_w_o.astype(BF16)
    w_in_all, lru_wo_all = lru_w_in.astype(BF16), lru_w_o.astype(BF16)
    conv_b_all = lru_conv_b.reshape(lru_conv_b.shape[0], 1, lru_width)

    xl = x.reshape(bsz * n_tok, d)
    xc = ctx.reshape(bsz * n_ctx, d)
    for i in range(depth):
        last = i == depth - 1
        j = i // n_mixers
        mod = (mods, i, 0, True)
        mod_c = (mods, i, bsz, False)

        def half_ffn(u2d, rows, m, s, mix=None):
            return _ffn(u2d, rows, m, i, norm_pre, norm_post, wg_all, wu_all, wd_all, s, mix)

        xl = half_ffn(xl, n_tok, mod, 0)
        xc = half_ffn(xc, n_ctx, mod_c, 0)
        if i % n_mixers == 0:
            wo_all = na_wo_all
            q, k, v = _qkv(xl, n_tok, mod, i, norm_pre, cos_t, sin_t, w_qkv_all, j)
            qc, kc, vc = _qkv(xc, n_ctx, mod_c, i, norm_pre, ones_t, zeros_t, w_qkv_all, j)
            shp, shp_c = (bsz, n_tok, d), (bsz, n_ctx, d)
            kc3, vc3 = kc.reshape(shp_c), vc.reshape(shp_c)
            bias_tabs = _attn_bias_tables(na_rpb[j], n_tok // GRID_W)
            act = _attention(q.reshape(shp), k.reshape(shp), v.reshape(shp), kc3, vc3, bias_tabs)
            act = act.reshape(bsz * n_tok, d)
            if not last:
                act_c = _ctx_attention(qc.reshape(shp_c), kc3, vc3).reshape(bsz * n_ctx, d)
        else:
            wo_all = lru_wo_all
            wcat = (0.5 * jnp.concatenate([lru_w_a[j], lru_w_x[j]], axis=-1)).astype(BF16)
            half_b_a, half_b_x = 0.5 * lru_b_a[j], 0.5 * lru_b_x[j]
            u_l, g_l = _lru_in(xl.reshape(bsz, n_tok, d), mod, i, norm_pre, w_in_all,
                               lru_conv_w, conv_b_all, j)
            u_c, g_c = _lru_in(xc.reshape(bsz, n_ctx, d), mod_c, i, norm_pre, w_in_all,
                               lru_conv_w, conv_b_all, j)
            zero_state = jnp.zeros((bsz, 1, lru_width), F32)
            act_c, fin_f, fin_b = _bidirectional_lru(u_c, g_c, wcat, half_b_a, half_b_x,
                                                     lru_lambda[j], zero_state, zero_state)
            act, _, _ = _bidirectional_lru(u_l, g_l, wcat, half_b_a, half_b_x,
                                           lru_lambda[j], fin_f, fin_b)
            act = act.reshape(bsz * n_tok, lru_width)
            act_c = act_c.reshape(bsz * n_ctx, lru_width)
        xl = half_ffn(xl, n_tok, mod, 2, mix=(act, wo_all, j))
        if not last:
            xc = half_ffn(xc, n_ctx, mod_c, 2, mix=(act_c, wo_all, j))
    return xl.reshape(bsz, n_tok, d)
```

```python
import functools

import numpy as np
import jax
import jax.numpy as jnp
from jax import lax
from jax.experimental import pallas as pl
from jax.experimental.pallas import tpu as pltpu

F32 = jnp.float32
BF16 = jnp.bfloat16

GRID_W = 64
NA_HEADS = 16
NA_WIN_ROWS = 8
NA_WIN_COLS = 16
ROPE_BASE = 10000.0
LRU_C = 8.0
CONV_W = 4
CONV_PAD_LEFT = 2
RMS_EPS = 1e-6
FFN_RES_WEIGHT = 0.5

LANES = 128
SUBLANES = 8
MXU_DIM = 256

FFN_ROWS = 1024
FFN_SUB_ROWS = 512
FFN_HID_CHUNK = MXU_DIM
PROJ_ROWS = 512
ATT_ROWS = 4
ATT_PAIRS = 2
SCAN_ROWS = 512
MOD_COLS = 1024
VMEM_LIMIT = 56 * 1024 * 1024
NEG_BIG = -1e30
SQRT_TINY = 1e-30
LOG2_E = float(np.log2(np.e))


def _cparams(sem):
    return pltpu.CompilerParams(dimension_semantics=sem, vmem_limit_bytes=VMEM_LIMIT)


def _const_spec(shape):
    nd = len(shape)
    return pl.BlockSpec(shape, lambda *_: (0,) * nd, pipeline_mode=pl.Buffered(1))


def _layer_spec(arr, lead, cols=None):
    tail = arr.shape[len(lead):]
    if cols is not None:
        tail = tail[:-1] + (cols[0],)
    idx = tuple(lead) + (0,) * (len(tail) - 1) + (0 if cols is None else cols[1],)
    return pl.BlockSpec((None,) * len(lead) + tail, lambda *_: idx, pipeline_mode=pl.Buffered(1))


def _mod_spec(mod, batch_of):
    table, layer, row, per_batch = mod
    if per_batch:
        index_map = lambda *g: (layer, row + batch_of(*g), 0, 0)
    else:
        index_map = lambda *g: (layer, row, 0, 0)
    return pl.BlockSpec((None, None) + table.shape[2:], index_map)


def _rms(x, gain):
    ms = jnp.mean(x * x, axis=-1, keepdims=True)
    return x * lax.rsqrt(ms + RMS_EPS) * gain


def _modulate(x, npre, mod, s):
    return _rms(x, npre[s:s + 1]) * (1.0 + mod[3 * s + 1:3 * s + 2]) + mod[3 * s:3 * s + 1]


def _mod_kernel(c_ref, w_ref, b_ref, o_ref):
    c = c_ref[...]
    s = (c * jax.nn.sigmoid(c)).astype(BF16)
    o_ref[...] = jnp.dot(s, w_ref[...].astype(BF16), preferred_element_type=F32) + b_ref[...]


def _adaln(cond, ada_w, ada_b):
    depth, d, n = ada_w.shape
    rows = cond.shape[0]
    return pl.pallas_call(
        _mod_kernel,
        grid=(depth, n // MOD_COLS),
        in_specs=[
            pl.BlockSpec((rows, d), lambda l, j: (0, 0)),
            pl.BlockSpec((None, d, MOD_COLS), lambda l, j: (l, 0, j)),
            pl.BlockSpec((None, 1, MOD_COLS), lambda l, j: (l, 0, j)),
        ],
        out_specs=pl.BlockSpec((None, rows, MOD_COLS), lambda l, j: (l, 0, j)),
        out_shape=jax.ShapeDtypeStruct((depth, rows, n), F32),
        compiler_params=_cparams(("arbitrary", "arbitrary")),
        name="adaln_mod",
    )(cond, ada_w, ada_b.reshape(depth, 1, n))


def _ffn_kernel(*refs, s, has_mix):
    if has_mix:
        x_ref, a_ref, wo_ref, mod_ref, npre_ref, npost_ref, wg_ref, wu_ref, wd_ref, o_ref = refs
    else:
        x_ref, mod_ref, npre_ref, npost_ref, wg_ref, wu_ref, wd_ref, o_ref = refs
    mod = mod_ref[...]
    npre = npre_ref[...]
    npost = npost_ref[...]
    hid = wg_ref.shape[1]
    sub = min(FFN_SUB_ROWS, x_ref.shape[0])
    blocks = [slice(r, r + sub) for r in range(0, x_ref.shape[0], sub)]

    def prologue(rows):
        x = x_ref[rows, :]
        if has_mix:
            y = jnp.dot(a_ref[rows, :], wo_ref[...], preferred_element_type=F32)
            x = x + mod[5:6] * _rms(y, npost[1:2])
        return x, _modulate(x, npre, mod, s).astype(BF16)

    def swiglu(rows, x, h):
        acc = None
        for c in range(0, hid, FFN_HID_CHUNK):
            g = jnp.dot(h, wg_ref[:, c:c + FFN_HID_CHUNK], preferred_element_type=F32)
            u = jnp.dot(h, wu_ref[:, c:c + FFN_HID_CHUNK], preferred_element_type=F32)
            a = (g * jax.nn.sigmoid(g) * u).astype(BF16)
            part = jnp.dot(a, wd_ref[c:c + FFN_HID_CHUNK, :], preferred_element_type=F32)
            acc = part if acc is None else acc + part
        o_ref[rows, :] = x + FFN_RES_WEIGHT * mod[3 * s + 2:3 * s + 3] * _rms(acc, npost[s:s + 1])

    ready = prologue(blocks[0])
    for k, rows in enumerate(blocks):
        upcoming = prologue(blocks[k + 1]) if k + 1 < len(blocks) else None
        swiglu(rows, *ready)
        ready = upcoming


def _ffn(x2d, rows_per_batch, mod, layer, norm_pre, norm_post, wg, wu, wd, s, mix=None):
    n, d = x2d.shape
    tm = min(FFN_ROWS, n)
    half = (layer, s // 2)
    row_spec = pl.BlockSpec((tm, d), lambda i: (i, 0))
    in_specs = [row_spec]
    args = [x2d]
    if mix is not None:
        a2d, wo, wo_layer = mix
        in_specs += [pl.BlockSpec((tm, a2d.shape[1]), lambda i: (i, 0)), _layer_spec(wo, (wo_layer,))]
        args += [a2d, wo]
    in_specs += [_mod_spec(mod, lambda i: (i * tm) // rows_per_batch),
                 _layer_spec(norm_pre, (layer,)), _layer_spec(norm_post, (layer,)),
                 _layer_spec(wg, half), _layer_spec(wu, half), _layer_spec(wd, half)]
    args += [mod[0], norm_pre, norm_post, wg, wu, wd]
    return pl.pallas_call(
        functools.partial(_ffn_kernel, s=s, has_mix=mix is not None),
        grid=(n // tm,),
        in_specs=in_specs,
        out_specs=row_spec,
        out_shape=jax.ShapeDtypeStruct((n, d), F32),
        compiler_params=_cparams(("arbitrary",)),
        name="mix_ffn" if mix is not None else "ffn",
    )(*args)


def _qkv_kernel(x_ref, mod_ref, npre_ref, cos_ref, sin_ref, w_ref, q_ref, k_ref, v_ref, *, scale):
    d = x_ref.shape[1]
    half = d // NA_HEADS // 2
    h = _modulate(x_ref[...], npre_ref[...], mod_ref[...], 1).astype(BF16)
    reps = d // cos_ref.shape[1]
    cosf = jnp.tile(cos_ref[...], (1, reps))
    sinf = jnp.tile(sin_ref[...], (1, reps))
    lane = lax.broadcasted_iota(jnp.int32, (x_ref.shape[0], d), 1)
    first = (lane & (2 * half - 1)) < half

    def rope(t):
        partner = jnp.where(first, pltpu.roll(t, d - half, 1), pltpu.roll(t, half, 1))
        return t * cosf + partner * sinf

    q = jnp.dot(h, w_ref[:, 0:d], preferred_element_type=F32)
    q_ref[...] = (rope(q) * scale).astype(BF16)
    k = jnp.dot(h, w_ref[:, d:2 * d], preferred_element_type=F32)
    k_ref[...] = rope(k).astype(BF16)
    v = jnp.dot(h, w_ref[:, 2 * d:3 * d], preferred_element_type=F32)
    v_ref[...] = v.astype(BF16)


def _qkv(x2d, rows_per_batch, mod, layer, norm_pre, cos_t, sin_t, w_qkv, w_layer):
    n, d = x2d.shape
    tm = min(PROJ_ROWS, rows_per_batch)
    tiles_per_batch = rows_per_batch // tm
    row_spec = pl.BlockSpec((tm, d), lambda i: (i, 0))
    tab_spec = pl.BlockSpec((tm, cos_t.shape[1]), lambda i: (i % tiles_per_batch, 0))
    out = jax.ShapeDtypeStruct((n, d), BF16)
    return pl.pallas_call(
        functools.partial(_qkv_kernel, scale=float((d // NA_HEADS) ** -0.5) * LOG2_E),
        grid=(n // tm,),
        in_specs=[row_spec, _mod_spec(mod, lambda i: i // tiles_per_batch),
                  _layer_spec(norm_pre, (layer,)), tab_spec, tab_spec, _layer_spec(w_qkv, (w_layer,))],
        out_specs=[row_spec, row_spec, row_spec],
        out_shape=[out, out, out],
        compiler_params=_cparams(("arbitrary",)),
        name="qkv_rope",
    )(x2d, mod[0], norm_pre, cos_t, sin_t, w_qkv)


_NT = (((1,), (1,)), ((), ()))


def _lane_is_head0(shape):
    return lax.broadcasted_iota(jnp.int32, shape, len(shape) - 1) < shape[-1] // 2


def _head_scores(q, hd, kv_blocks, bias):
    is0 = _lane_is_head0(q.shape)
    qh = jnp.where(is0 if hd == 0 else jnp.logical_not(is0), q, jnp.zeros_like(q))
    scores = []
    for blk, (k, _) in enumerate(kv_blocks):
        s = lax.dot_general(qh, k, _NT, preferred_element_type=F32)
        if blk == 0 and bias is not None:
            s = s + bias
        scores.append(s)
    return scores


def _head_output(scores, hd, kv_blocks):
    m = functools.reduce(jnp.maximum, [jnp.max(s, axis=-1, keepdims=True) for s in scores])
    o = None
    for s, (_, v) in zip(scores, kv_blocks):
        is0 = _lane_is_head0(v.shape)
        v_aug = jnp.where(is0 if hd == 0 else jnp.logical_not(is0), v, jnp.ones_like(v))
        part = jnp.dot(jnp.exp2(s - m).astype(BF16), v_aug, preferred_element_type=F32)
        o = part if o is None else o + part
    return o


def _attend(jobs):
    raw = []
    pending = None
    for q, hd, kv_blocks, bias in jobs:
        scores = _head_scores(q, hd, kv_blocks, bias)
        if pending is not None:
            raw.append(_head_output(*pending))
        pending = (scores, hd, kv_blocks)
    raw.append(_head_output(*pending))
    outs = []
    for o0, o1 in zip(raw[0::2], raw[1::2]):
        is0 = _lane_is_head0(o0.shape)
        num = jnp.where(is0, o0, o1)
        den = pltpu.roll(jnp.where(is0, o1, o0), o0.shape[-1] // 2, 1)
        outs.append(num / den)
    return outs


def _attn_tile_plan(n_rows):
    win_rows = min(NA_WIN_ROWS, n_rows)
    key_rows = ATT_ROWS + NA_WIN_ROWS - 1
    n_blocks = n_rows // ATT_ROWS
    masked = 2 * NA_WIN_ROWS - 1
    plan = np.full((3, ATT_ROWS, key_rows), masked, np.int64)
    for kind, blk in enumerate((0, 1, n_blocks - 1)):
        r_b = blk * ATT_ROWS
        w0 = int(np.clip(r_b - NA_WIN_ROWS // 2, 0, n_rows - key_rows))
        for ri in range(ATT_ROWS):
            r = r_b + ri
            r0 = int(np.clip(r - win_rows // 2, 0, n_rows - win_rows))
            for kri in range(key_rows):
                kr = w0 + kri
                if r0 <= kr < r0 + win_rows:
                    plan[kind, ri, kri] = kr - r + NA_WIN_ROWS - 1
    return plan


def _attn_kernel(q_ref, k_ref, v_ref, kc_ref, vc_ref, band_ref, o_ref, bias_s, *, n_rows, plan):
    nk = bias_s.shape[-1]
    rb = pl.program_id(2)
    n_blocks = pl.num_programs(2)

    def build(kind):
        for h in range(band_ref.shape[0]):
            for ri in range(ATT_ROWS):
                tiles = [band_ref[h, int(e)] for e in plan[kind, ri]]
                bias_s[h // 2, h % 2, ri * GRID_W:(ri + 1) * GRID_W, :] = jnp.concatenate(tiles, axis=1)

    for kind, blk in enumerate((0, 1, n_blocks - 1)):
        pl.when(rb == blk)(functools.partial(build, kind))

    w0 = jnp.clip(rb * ATT_ROWS - NA_WIN_ROWS // 2, 0, n_rows - nk // GRID_W)
    start = pl.multiple_of(w0 * GRID_W, GRID_W)
    jobs = []
    for pp in range(bias_s.shape[0]):
        lanes = slice(pp * LANES, (pp + 1) * LANES)
        kv_blocks = [(k_ref[pl.ds(start, nk), lanes], v_ref[pl.ds(start, nk), lanes]),
                     (kc_ref[:, lanes], vc_ref[:, lanes])]
        q = q_ref[:, lanes]
        jobs += [(q, hd, kv_blocks, bias_s[pp, hd]) for hd in range(2)]
    for pp, o in enumerate(_attend(jobs)):
        o_ref[:, pp * LANES:(pp + 1) * LANES] = o.astype(o_ref.dtype)


def _attn_band_tiles(rpb):
    heads, n_dr, n_dc = rpb.shape
    win_cols = min(NA_WIN_COLS, GRID_W)
    lead = GRID_W - NA_WIN_COLS
    padded = jnp.pad(rpb.astype(F32) * LOG2_E, ((0, 0), (0, 0), (lead, 2 * GRID_W - 1 - lead - n_dc)))
    band = jnp.stack([padded[:, :, GRID_W - 1 - qc:2 * GRID_W - 1 - qc] for qc in range(GRID_W)], axis=2)
    col = np.arange(GRID_W)
    c0 = np.clip(col - win_cols // 2, 0, GRID_W - win_cols)
    valid_c = (col[None, :] >= c0[:, None]) & (col[None, :] < c0[:, None] + win_cols)
    band = jnp.where(jnp.asarray(valid_c)[None, None], band, NEG_BIG)
    masked = jnp.full((heads, 1, GRID_W, GRID_W), NEG_BIG, F32)
    return jnp.concatenate([band, masked], axis=1)


def _attention(q, k, v, kc, vc, band):
    bsz, n_tok, d = q.shape
    n_ctx = kc.shape[1]
    n_rows = n_tok // GRID_W
    n_blocks = n_rows // ATT_ROWS
    rq = ATT_ROWS * GRID_W
    nk = (ATT_ROWS + NA_WIN_ROWS - 1) * GRID_W
    pair = ATT_PAIRS * LANES
    q_spec = pl.BlockSpec((None, rq, pair), lambda b, hp, rb: (b, rb, hp))
    seq_spec = pl.BlockSpec((None, n_tok, pair), lambda b, hp, rb: (b, 0, hp))
    ctx_spec = pl.BlockSpec((None, n_ctx, pair), lambda b, hp, rb: (b, 0, hp))
    band_spec = pl.BlockSpec((2 * ATT_PAIRS,) + band.shape[1:], lambda b, hp, rb: (hp, 0, 0, 0))
    return pl.pallas_call(
        functools.partial(_attn_kernel, n_rows=n_rows, plan=_attn_tile_plan(n_rows)),
        grid=(bsz, d // pair, n_blocks),
        in_specs=[q_spec, seq_spec, seq_spec, ctx_spec, ctx_spec, band_spec],
        out_specs=q_spec,
        out_shape=jax.ShapeDtypeStruct((bsz, n_tok, d), BF16),
        scratch_shapes=[pltpu.VMEM((ATT_PAIRS, 2, rq, nk), F32)],
        compiler_params=_cparams(("arbitrary", "arbitrary", "arbitrary")),
        name="nbr_attention",
    )(q, k, v, kc, vc, band)


def _ctx_attn_kernel(q_ref, k_ref, v_ref, o_ref):
    q = q_ref[...]
    kv_blocks = [(k_ref[...], v_ref[...])]
    (o,) = _attend([(q, hd, kv_blocks, None) for hd in range(2)])
    o_ref[...] = o.astype(o_ref.dtype)


def _ctx_attention(qc, kc, vc):
    bsz, n_ctx, d = qc.shape
    pair = 2 * (d // NA_HEADS)
    spec = pl.BlockSpec((None, n_ctx, pair), lambda b, hp: (b, 0, hp))
    return pl.pallas_call(
        _ctx_attn_kernel,
        grid=(bsz, d // pair),
        in_specs=[spec, spec, spec],
        out_specs=spec,
        out_shape=jax.ShapeDtypeStruct((bsz, n_ctx, d), BF16),
        compiler_params=_cparams(("arbitrary", "arbitrary")),
        name="ctx_attention",
    )(qc, kc, vc)


def _gelu_tanh(x):
    c = float(np.sqrt(2.0 / np.pi))
    return 0.5 * x * (1.0 + jnp.tanh(c * (x + 0.044715 * (x * x * x))))


def _lru_in_kernel(x_ref, xp_ref, xn_ref, mod_ref, npre_ref, wg_ref, wr_ref, cw_ref, cb_ref,
                   u_ref, g_ref, rec_s):
    tm = x_ref.shape[0]
    halo = xp_ref.shape[0]
    i = pl.program_id(1)
    last = pl.num_programs(1) - 1
    mod = mod_ref[...]
    npre = npre_ref[...]
    x_ext = jnp.concatenate([xp_ref[...], x_ref[...], xn_ref[...]], axis=0)
    h_ext = _modulate(x_ext, npre, mod, 1)
    rec = jnp.dot(h_ext.astype(BF16), wr_ref[...], preferred_element_type=F32)
    gate = jnp.dot(h_ext[halo:halo + tm].astype(BF16), wg_ref[...], preferred_element_type=F32)
    row = lax.broadcasted_iota(jnp.int32, (tm + 2 * halo, 1), 0)
    valid = ((row >= halo) | (i > 0)) & ((row < halo + tm) | (i < last))
    rec = jnp.where(valid, rec, 0.0)
    cw = cw_ref[...]
    cb = cb_ref[...]
    n_blk, _, blk_w = u_ref.shape
    seg = tm // SUBLANES
    for n in range(n_blk):
        cols = slice(n * blk_w, (n + 1) * blk_w)
        rec_s[n] = rec[:, cols]
        u = cb[:, cols]
        for tap in range(CONV_W):
            u = u + rec_s[n, pl.ds(halo - CONV_PAD_LEFT + tap, tm), :] * cw[tap:tap + 1, cols]
        for j in range(SUBLANES):
            u_ref[n, pl.ds(j, seg, stride=SUBLANES), :] = u[j * seg:(j + 1) * seg]
    g_ref[...] = _gelu_tanh(gate).astype(g_ref.dtype)


def _lru_in(x3d, mod, layer, norm_pre, w_in, conv_w, conv_b, w_layer):
    bsz, n_tok, d = x3d.shape
    width = w_in.shape[2] // 2
    tm = min(SCAN_ROWS, n_tok)
    n_blk = width // LANES
    halo = SUBLANES
    halo_blocks = n_tok // halo
    per_tile = tm // halo
    out_spec = pl.BlockSpec((None, tm, width), lambda b, i: (b, i, 0))
    u_spec = pl.BlockSpec((None, None, n_blk, tm, LANES), lambda b, i: (b, i, 0, 0, 0))
    return pl.pallas_call(
        _lru_in_kernel,
        grid=(bsz, n_tok // tm),
        in_specs=[
            pl.BlockSpec((None, tm, d), lambda b, i: (b, i, 0)),
            pl.BlockSpec((None, halo, d), lambda b, i: (b, jnp.maximum(i * per_tile - 1, 0), 0)),
            pl.BlockSpec((None, halo, d),
                         lambda b, i: (b, jnp.minimum((i + 1) * per_tile, halo_blocks - 1), 0)),
            _mod_spec(mod, lambda b, i: b),
            _layer_spec(norm_pre, (layer,)),
            _layer_spec(w_in, (w_layer,), cols=(width, 0)), _layer_spec(w_in, (w_layer,), cols=(width, 1)),
            _layer_spec(conv_w, (w_layer,)), _layer_spec(conv_b, (w_layer,)),
        ],
        out_specs=[u_spec, out_spec],
        out_shape=[jax.ShapeDtypeStruct((bsz, n_tok // tm, n_blk, tm, LANES), F32),
                   jax.ShapeDtypeStruct((bsz, n_tok, width), BF16)],
        scratch_shapes=[pltpu.VMEM((n_blk, tm + 2 * halo, LANES), F32)],
        compiler_params=_cparams(("arbitrary", "arbitrary")),
        name="lru_in",
    )(x3d, x3d, x3d, mod[0], norm_pre, w_in, w_in, conv_w, conv_b)


def _scan_kernel(*refs, reverse, combine):
    if combine:
        (u_ref, wcat_ref, ba_ref, bx_ref, lam_ref, init_ref, prev_ref, g_ref,
         o_ref, fin_ref, h_s, p_s, carry_s) = refs
    else:
        (u_ref, wcat_ref, ba_ref, bx_ref, lam_ref, init_ref,
         o_ref, fin_ref, h_s, p_s, carry_s) = refs
    n_blk, tt, blk_w = u_ref.shape
    seg = tt // SUBLANES

    @pl.when(pl.program_id(1) == 0)
    def _():
        carry_s[...] = init_ref[...]

    lam = lam_ref[...]
    softplus = jnp.maximum(-lam, 0.0) + jnp.log1p(jnp.exp(-jnp.abs(lam)))
    half_log2_decay = (-0.5 * LRU_C * LOG2_E) * softplus
    segments = range(SUBLANES - 1, -1, -1) if reverse else range(SUBLANES)

    for n in range(n_blk):
        cols = slice(n * blk_w, (n + 1) * blk_w)
        un = u_ref[n]
        z = jnp.dot(un.astype(BF16), wcat_ref[n], preferred_element_type=F32)
        t_r = jnp.tanh(z[:, :blk_w] + ba_ref[:, cols])
        t_i = jnp.tanh(z[:, blk_w:] + bx_ref[:, cols])
        k = half_log2_decay[:, cols]
        a = jnp.exp2(k + k * t_r)
        one_m = 1.0 - a * a
        p_s[n] = a
        h_s[n] = (one_m * lax.rsqrt(jnp.maximum(one_m, SQRT_TINY))) * ((0.5 + 0.5 * t_i) * un)

    def scan_step(t, carry):
        i = (seg - 1 - t) if reverse else t
        tile = pl.ds(pl.multiple_of(i * SUBLANES, SUBLANES), SUBLANES)
        hs, ps = carry
        new_h, new_p = [], []
        for n in range(n_blk):
            a_t = p_s[n, tile, :]
            h = a_t * hs[n] + h_s[n, tile, :]
            p = a_t * ps[n]
            h_s[n, tile, :] = h
            p_s[n, tile, :] = p
            new_h.append(h)
            new_p.append(p)
        return tuple(new_h), tuple(new_p)

    zeros = tuple(jnp.zeros((SUBLANES, blk_w), F32) for _ in range(n_blk))
    ones = tuple(jnp.ones((SUBLANES, blk_w), F32) for _ in range(n_blk))
    h_end, p_end = lax.fori_loop(0, seg, scan_step, (zeros, ones), unroll=4)

    for n in range(n_blk):
        cols = slice(n * blk_w, (n + 1) * blk_w)
        h, p = h_end[n], p_end[n]
        state = carry_s[:, cols]
        entry = [None] * SUBLANES
        for j in segments:
            entry[j] = state
            state = h[j:j + 1] + p[j:j + 1] * state
        carry_s[:, cols] = state
        for j in range(SUBLANES):
            rows = slice(j * seg, (j + 1) * seg)
            picked = pl.ds(j, seg, stride=SUBLANES)
            ht = h_s[n, picked, :] + p_s[n, picked, :] * entry[j]
            if combine:
                ht = (ht + prev_ref[rows, cols]) * g_ref[rows, cols].astype(F32)
            o_ref[rows, cols] = ht.astype(o_ref.dtype)

    fin_ref[...] = carry_s[...]


def _scan(u, width, wcat, b_a, b_x, lam, init, reverse, prev=None, gate=None):
    bsz, n_chunks, n_blk, tt, blk_w = u.shape
    n_tok = n_chunks * tt
    combine = prev is not None
    if reverse:
        chunk = lambda i: n_chunks - 1 - i
    else:
        chunk = lambda i: i
    u_spec = pl.BlockSpec((None, None, n_blk, tt, blk_w), lambda b, i: (b, chunk(i), 0, 0, 0))
    seq_spec = pl.BlockSpec((None, tt, width), lambda b, i: (b, chunk(i), 0))
    vec_spec = pl.BlockSpec((None, 1, width), lambda b, i: (b, 0, 0))
    in_specs = [u_spec, _const_spec(wcat.shape), _const_spec((1, width)), _const_spec((1, width)),
                _const_spec((1, width)), vec_spec]
    args = [u, wcat, b_a.reshape(1, width), b_x.reshape(1, width), lam.reshape(1, width), init]
    if combine:
        in_specs += [seq_spec, seq_spec]
        args += [prev, gate]
    return pl.pallas_call(
        functools.partial(_scan_kernel, reverse=reverse, combine=combine),
        grid=(bsz, n_chunks),
        in_specs=in_specs,
        out_specs=[seq_spec, vec_spec],
        out_shape=[jax.ShapeDtypeStruct((bsz, n_tok, width), BF16 if combine else F32),
                   jax.ShapeDtypeStruct((bsz, 1, width), F32)],
        scratch_shapes=[pltpu.VMEM((n_blk, tt, blk_w), F32), pltpu.VMEM((n_blk, tt, blk_w), F32),
                        pltpu.VMEM((1, width), F32)],
        compiler_params=_cparams(("arbitrary", "arbitrary")),
        name="lru_scan_bwd" if reverse else "lru_scan_fwd",
    )(*args)


def _bidirectional_lru(u, gate, wcat, b_a, b_x, lam, init_f, init_b):
    width = gate.shape[-1]
    h_f, fin_f = _scan(u, width, wcat[0], b_a[0], b_x[0], lam[0], init_f, reverse=False)
    act, fin_b = _scan(u, width, wcat[1], b_a[1], b_x[1], lam[1], init_b, reverse=True,
                       prev=h_f, gate=gate)
    return act, fin_f, fin_b


def _rope_tables(n_tok, head_dim):
    pairs = head_dim // 4
    t = jnp.arange(n_tok, dtype=jnp.int32)
    row = (t // GRID_W).astype(F32)
    col = (t % GRID_W).astype(F32)
    inv_freq = jnp.power(ROPE_BASE, -jnp.arange(pairs, dtype=F32) / pairs)
    ang = jnp.concatenate([row[:, None] * inv_freq, col[:, None] * inv_freq], axis=-1)
    cos, sin = jnp.cos(ang), jnp.sin(ang)
    cos_h = jnp.concatenate([cos, cos], axis=-1)
    sin_h = jnp.concatenate([-sin, sin], axis=-1)
    return jnp.tile(cos_h, (1, 2)), jnp.tile(sin_h, (1, 2))


def kernel(x, c, ctx, c_ctx, ada_w, ada_b, norm_pre, norm_post, ffn_w_gate, ffn_w_up, ffn_w_down,
           na_w_qkv, na_w_o, na_rpb, lru_w_in, lru_conv_w, lru_conv_b, lru_w_a, lru_b_a, lru_w_x,
           lru_b_x, lru_lambda, lru_w_o):
    bsz, n_tok, d = x.shape
    n_ctx = ctx.shape[1]
    depth = ada_w.shape[0]
    n_mod = ada_w.shape[2] // d
    head_dim = d // NA_HEADS
    lru_width = lru_w_o.shape[1]
    n_mixers = 2

    cond = jnp.concatenate([c, c_ctx[None, :], jnp.zeros((SUBLANES - bsz - 1, d), F32)], axis=0)
    mods = _adaln(cond, ada_w, ada_b).reshape(depth, SUBLANES, n_mod, d)

    cos_t, sin_t = _rope_tables(n_tok, head_dim)
    ones_t = jnp.ones((n_ctx, cos_t.shape[1]), F32)
    zeros_t = jnp.zeros((n_ctx, cos_t.shape[1]), F32)

    wg_all, wu_all, wd_all = (w.astype(BF16) for w in (ffn_w_gate, ffn_w_up, ffn_w_down))
    w_qkv_all, na_wo_all = na_w_qkv.astype(BF16), na_w_o.astype(BF16)
    w_in_all, lru_wo_all = lru_w_in.astype(BF16), lru_w_o.astype(BF16)
    conv_b_all = lru_conv_b.reshape(lru_conv_b.shape[0], 1, lru_width)

    xl = x.reshape(bsz * n_tok, d)
    xc = ctx.reshape(bsz * n_ctx, d)
    for i in range(depth):
        last = i == depth - 1
        j = i // n_mixers
        mod = (mods, i, 0, True)
        mod_c = (mods, i, bsz, False)

        def half_ffn(u2d, rows, m, s, mix=None):
            return _ffn(u2d, rows, m, i, norm_pre, norm_post, wg_all, wu_all, wd_all, s, mix)

        xl = half_ffn(xl, n_tok, mod, 0)
        xc = half_ffn(xc, n_ctx, mod_c, 0)
        if i % n_mixers == 0:
            wo_all = na_wo_all
            q, k, v = _qkv(xl, n_tok, mod, i, norm_pre, cos_t, sin_t, w_qkv_all, j)
            qc, kc, vc = _qkv(xc, n_ctx, mod_c, i, norm_pre, ones_t, zeros_t, w_qkv_all, j)
            shp, shp_c = (bsz, n_tok, d), (bsz, n_ctx, d)
            kc3, vc3 = kc.reshape(shp_c), vc.reshape(shp_c)
            band = _attn_band_tiles(na_rpb[j])
            act = _attention(q.reshape(shp), k.reshape(shp), v.reshape(shp), kc3, vc3, band)
            act = act.reshape(bsz * n_tok, d)
            if not last:
                act_c = _ctx_attention(qc.reshape(shp_c), kc3, vc3).reshape(bsz * n_ctx, d)
        else:
            wo_all = lru_wo_all
            wcat = (0.5 * jnp.concatenate([lru_w_a[j], lru_w_x[j]], axis=-1)).astype(BF16)
            half_b_a, half_b_x = 0.5 * lru_b_a[j], 0.5 * lru_b_x[j]
            u_l, g_l = _lru_in(xl.reshape(bsz, n_tok, d), mod, i, norm_pre, w_in_all,
                               lru_conv_w, conv_b_all, j)
            u_c, g_c = _lru_in(xc.reshape(bsz, n_ctx, d), mod_c, i, norm_pre, w_in_all,
                               lru_conv_w, conv_b_all, j)
            zero_state = jnp.zeros((bsz, 1, lru_width), F32)
            act_c, fin_f, fin_b = _bidirectional_lru(u_c, g_c, wcat, half_b_a, half_b_x,
                                                     lru_lambda[j], zero_state, zero_state)
            act, _, _ = _bidirectional_lru(u_l, g_l, wcat, half_b_a, half_b_x,
                                           lru_lambda[j], fin_f, fin_b)
            act = act.reshape(bsz * n_tok, lru_width)
            act_c = act_c.reshape(bsz * n_ctx, lru_width)
        xl = half_ffn(xl, n_tok, mod, 2, mix=(act, wo_all, j))
        if not last:
            xc = half_ffn(xc, n_ctx, mod_c, 2, mix=(act_c, wo_all, j))
    return xl.reshape(bsz, n_tok, d)
```

```python
import functools

import numpy as np
import jax
import jax.numpy as jnp
from jax import lax
from jax.experimental import pallas as pl
from jax.experimental.pallas import tpu as pltpu

F32 = jnp.float32
BF16 = jnp.bfloat16

GRID_W = 64
NA_HEADS = 16
NA_WIN_ROWS = 8
NA_WIN_COLS = 16
ROPE_BASE = 10000.0
LRU_C = 8.0
CONV_W = 4
CONV_PAD_LEFT = 2
RMS_EPS = 1e-6
FFN_RES_WEIGHT = 0.5

LANES = 128
SUBLANES = 8
MXU_DIM = 256

FFN_ROWS = 1024
FFN_SUB_ROWS = 512
FFN_HID_CHUNK = MXU_DIM
PROJ_ROWS = 1024
PROJ_SUB_ROWS = 512
ATT_ROWS = 4
ATT_PAIRS = 2
SCAN_ROWS = 1024
MOD_COLS = 1024
VMEM_LIMIT = 56 * 1024 * 1024
NEG_BIG = -1e30
SQRT_TINY = 1e-30
LOG2_E = float(np.log2(np.e))


def _cparams(sem):
    return pltpu.CompilerParams(dimension_semantics=sem, vmem_limit_bytes=VMEM_LIMIT)


def _const_spec(shape):
    nd = len(shape)
    return pl.BlockSpec(shape, lambda *_: (0,) * nd, pipeline_mode=pl.Buffered(1))


def _layer_spec(arr, lead, cols=None):
    tail = arr.shape[len(lead):]
    if cols is not None:
        tail = tail[:-1] + (cols[0],)
    idx = tuple(lead) + (0,) * (len(tail) - 1) + (0 if cols is None else cols[1],)
    return pl.BlockSpec((None,) * len(lead) + tail, lambda *_: idx, pipeline_mode=pl.Buffered(1))


def _mod_spec(mod, batch_of):
    table, layer, row, per_batch = mod
    if per_batch:
        index_map = lambda *g: (layer, row + batch_of(*g), 0, 0)
    else:
        index_map = lambda *g: (layer, row, 0, 0)
    return pl.BlockSpec((None, None) + table.shape[2:], index_map)


def _rms(x, gain):
    ms = jnp.mean(x * x, axis=-1, keepdims=True)
    return x * lax.rsqrt(ms + RMS_EPS) * gain


def _modulate(x, npre, mod, s):
    return _rms(x, npre[s:s + 1]) * (1.0 + mod[3 * s + 1:3 * s + 2]) + mod[3 * s:3 * s + 1]


def _mod_kernel(c_ref, w_ref, b_ref, o_ref):
    c = c_ref[...]
    s = (c * jax.nn.sigmoid(c)).astype(BF16)
    o_ref[...] = jnp.dot(s, w_ref[...].astype(BF16), preferred_element_type=F32) + b_ref[...]


def _adaln(cond, ada_w, ada_b):
    depth, d, n = ada_w.shape
    rows = cond.shape[0]
    return pl.pallas_call(
        _mod_kernel,
        grid=(depth, n // MOD_COLS),
        in_specs=[
            pl.BlockSpec((rows, d), lambda l, j: (0, 0)),
            pl.BlockSpec((None, d, MOD_COLS), lambda l, j: (l, 0, j)),
            pl.BlockSpec((None, 1, MOD_COLS), lambda l, j: (l, 0, j)),
        ],
        out_specs=pl.BlockSpec((None, rows, MOD_COLS), lambda l, j: (l, 0, j)),
        out_shape=jax.ShapeDtypeStruct((depth, rows, n), F32),
        compiler_params=_cparams(("arbitrary", "arbitrary")),
        name="adaln_mod",
    )(cond, ada_w, ada_b.reshape(depth, 1, n))


def _ffn_kernel(*refs, s, has_mix):
    if has_mix:
        x_ref, a_ref, wo_ref, mod_ref, npre_ref, npost_ref, wg_ref, wu_ref, wd_ref, o_ref = refs
    else:
        x_ref, mod_ref, npre_ref, npost_ref, wg_ref, wu_ref, wd_ref, o_ref = refs
    mod = mod_ref[...]
    npre = npre_ref[...]
    npost = npost_ref[...]
    hid = wg_ref.shape[1]
    sub = min(FFN_SUB_ROWS, x_ref.shape[0])
    blocks = [slice(r, r + sub) for r in range(0, x_ref.shape[0], sub)]

    def prologue(rows):
        x = x_ref[rows, :]
        if has_mix:
            y = jnp.dot(a_ref[rows, :], wo_ref[...], preferred_element_type=F32)
            x = x + mod[5:6] * _rms(y, npost[1:2])
        return x, _modulate(x, npre, mod, s).astype(BF16)

    def swiglu(rows, x, h):
        acc = None
        for c in range(0, hid, FFN_HID_CHUNK):
            g = jnp.dot(h, wg_ref[:, c:c + FFN_HID_CHUNK], preferred_element_type=F32)
            u = jnp.dot(h, wu_ref[:, c:c + FFN_HID_CHUNK], preferred_element_type=F32)
            a = (g * jax.nn.sigmoid(g) * u).astype(BF16)
            part = jnp.dot(a, wd_ref[c:c + FFN_HID_CHUNK, :], preferred_element_type=F32)
            acc = part if acc is None else acc + part
        o_ref[rows, :] = x + FFN_RES_WEIGHT * mod[3 * s + 2:3 * s + 3] * _rms(acc, npost[s:s + 1])

    ready = prologue(blocks[0])
    for k, rows in enumerate(blocks):
        upcoming = prologue(blocks[k + 1]) if k + 1 < len(blocks) else None
        swiglu(rows, *ready)
        ready = upcoming


def _ffn(x2d, rows_per_batch, mod, layer, norm_pre, norm_post, wg, wu, wd, s, mix=None):
    n, d = x2d.shape
    tm = min(FFN_ROWS, n)
    half = (layer, s // 2)
    row_spec = pl.BlockSpec((tm, d), lambda i: (i, 0))
    in_specs = [row_spec]
    args = [x2d]
    if mix is not None:
        a2d, wo, wo_layer = mix
        in_specs += [pl.BlockSpec((tm, a2d.shape[1]), lambda i: (i, 0)), _layer_spec(wo, (wo_layer,))]
        args += [a2d, wo]
    in_specs += [_mod_spec(mod, lambda i: (i * tm) // rows_per_batch),
                 _layer_spec(norm_pre, (layer,)), _layer_spec(norm_post, (layer,)),
                 _layer_spec(wg, half), _layer_spec(wu, half), _layer_spec(wd, half)]
    args += [mod[0], norm_pre, norm_post, wg, wu, wd]
    return pl.pallas_call(
        functools.partial(_ffn_kernel, s=s, has_mix=mix is not None),
        grid=(n // tm,),
        in_specs=in_specs,
        out_specs=row_spec,
        out_shape=jax.ShapeDtypeStruct((n, d), F32),
        compiler_params=_cparams(("arbitrary",)),
        name="mix_ffn" if mix is not None else "ffn",
    )(*args)


def _qkv_kernel(x_ref, mod_ref, npre_ref, cos_ref, sin_ref, w_ref, q_ref, k_ref, v_ref, *, scale):
    d = x_ref.shape[1]
    half = d // NA_HEADS // 2
    reps = d // cos_ref.shape[1]
    mod = mod_ref[...]
    npre = npre_ref[...]
    sub = min(PROJ_SUB_ROWS, x_ref.shape[0])
    blocks = [slice(r, r + sub) for r in range(0, x_ref.shape[0], sub)]
    lane = lax.broadcasted_iota(jnp.int32, (sub, d), 1)
    first = (lane & (2 * half - 1)) < half

    def rope(t, rows):
        cosf = jnp.tile(cos_ref[rows, :], (1, reps))
        sinf = jnp.tile(sin_ref[rows, :], (1, reps))
        partner = jnp.where(first, pltpu.roll(t, d - half, 1), pltpu.roll(t, half, 1))
        return t * cosf + partner * sinf

    hs = [_modulate(x_ref[rows, :], npre, mod, 1).astype(BF16) for rows in blocks]
    for rows, h in zip(blocks, hs):
        q = jnp.dot(h, w_ref[:, 0:d], preferred_element_type=F32)
        q_ref[rows, :] = (rope(q, rows) * scale).astype(BF16)
        k = jnp.dot(h, w_ref[:, d:2 * d], preferred_element_type=F32)
        k_ref[rows, :] = rope(k, rows).astype(BF16)
        v = jnp.dot(h, w_ref[:, 2 * d:3 * d], preferred_element_type=F32)
        v_ref[rows, :] = v.astype(BF16)


def _qkv(x2d, rows_per_batch, mod, layer, norm_pre, cos_t, sin_t, w_qkv, w_layer):
    n, d = x2d.shape
    tm = min(PROJ_ROWS, rows_per_batch)
    tiles_per_batch = rows_per_batch // tm
    row_spec = pl.BlockSpec((tm, d), lambda i: (i, 0))
    tab_spec = pl.BlockSpec((tm, cos_t.shape[1]), lambda i: (i % tiles_per_batch, 0))
    out = jax.ShapeDtypeStruct((n, d), BF16)
    return pl.pallas_call(
        functools.partial(_qkv_kernel, scale=float((d // NA_HEADS) ** -0.5) * LOG2_E),
        grid=(n // tm,),
        in_specs=[row_spec, _mod_spec(mod, lambda i: i // tiles_per_batch),
                  _layer_spec(norm_pre, (layer,)), tab_spec, tab_spec, _layer_spec(w_qkv, (w_layer,))],
        out_specs=[row_spec, row_spec, row_spec],
        out_shape=[out, out, out],
        compiler_params=_cparams(("arbitrary",)),
        name="qkv_rope",
    )(x2d, mod[0], norm_pre, cos_t, sin_t, w_qkv)


_NT = (((1,), (1,)), ((), ()))


def _lane_is_head0(shape):
    return lax.broadcasted_iota(jnp.int32, shape, len(shape) - 1) < shape[-1] // 2


def _head_scores(q, hd, kv_blocks, bias):
    is0 = _lane_is_head0(q.shape)
    qh = jnp.where(is0 if hd == 0 else jnp.logical_not(is0), q, jnp.zeros_like(q))
    scores = []
    for blk, (k, _) in enumerate(kv_blocks):
        s = lax.dot_general(qh, k, _NT, preferred_element_type=F32)
        if blk == 0 and bias is not None:
            s = s + bias
        scores.append(s)
    return scores


def _head_output(scores, hd, kv_blocks):
    m = functools.reduce(jnp.maximum, [jnp.max(s, axis=-1, keepdims=True) for s in scores])
    o = None
    for s, (_, v) in zip(scores, kv_blocks):
        is0 = _lane_is_head0(v.shape)
        v_aug = jnp.where(is0 if hd == 0 else jnp.logical_not(is0), v, jnp.ones_like(v))
        part = jnp.dot(jnp.exp2(s - m).astype(BF16), v_aug, preferred_element_type=F32)
        o = part if o is None else o + part
    return o


def _attend(jobs):
    raw = []
    pending = None
    for q, hd, kv_blocks, bias in jobs:
        scores = _head_scores(q, hd, kv_blocks, bias)
        if pending is not None:
            raw.append(_head_output(*pending))
        pending = (scores, hd, kv_blocks)
    raw.append(_head_output(*pending))
    outs = []
    for o0, o1 in zip(raw[0::2], raw[1::2]):
        is0 = _lane_is_head0(o0.shape)
        num = jnp.where(is0, o0, o1)
        den = pltpu.roll(jnp.where(is0, o1, o0), o0.shape[-1] // 2, 1)
        outs.append(num / den)
    return outs


def _attn_tile_plan(n_rows):
    win_rows = min(NA_WIN_ROWS, n_rows)
    key_rows = ATT_ROWS + NA_WIN_ROWS - 1
    n_blocks = n_rows // ATT_ROWS
    masked = 2 * NA_WIN_ROWS - 1
    plan = np.full((3, ATT_ROWS, key_rows), masked, np.int64)
    for kind, blk in enumerate((0, 1, n_blocks - 1)):
        r_b = blk * ATT_ROWS
        w0 = int(np.clip(r_b - NA_WIN_ROWS // 2, 0, n_rows - key_rows))
        for ri in range(ATT_ROWS):
            r = r_b + ri
            r0 = int(np.clip(r - win_rows // 2, 0, n_rows - win_rows))
            for kri in range(key_rows):
                kr = w0 + kri
                if r0 <= kr < r0 + win_rows:
                    plan[kind, ri, kri] = kr - r + NA_WIN_ROWS - 1
    return plan


def _attn_kernel(q_ref, k_ref, v_ref, kc_ref, vc_ref, band_ref, o_ref, bias_s, *, n_rows, plan):
    nk = bias_s.shape[-1]
    rb = pl.program_id(2)
    n_blocks = pl.num_programs(2)

    def build(kind):
        for h in range(band_ref.shape[0]):
            for ri in range(ATT_ROWS):
                tiles = [band_ref[h, int(e)] for e in plan[kind, ri]]
                bias_s[h // 2, h % 2, ri * GRID_W:(ri + 1) * GRID_W, :] = jnp.concatenate(tiles, axis=1)

    for kind, blk in enumerate((0, 1, n_blocks - 1)):
        pl.when(rb == blk)(functools.partial(build, kind))

    w0 = jnp.clip(rb * ATT_ROWS - NA_WIN_ROWS // 2, 0, n_rows - nk // GRID_W)
    start = pl.multiple_of(w0 * GRID_W, GRID_W)
    jobs = []
    for pp in range(bias_s.shape[0]):
        lanes = slice(pp * LANES, (pp + 1) * LANES)
        kv_blocks = [(k_ref[pl.ds(start, nk), lanes], v_ref[pl.ds(start, nk), lanes]),
                     (kc_ref[:, lanes], vc_ref[:, lanes])]
        q = q_ref[:, lanes]
        jobs += [(q, hd, kv_blocks, bias_s[pp, hd]) for hd in range(2)]
    for pp, o in enumerate(_attend(jobs)):
        o_ref[:, pp * LANES:(pp + 1) * LANES] = o.astype(o_ref.dtype)


def _attn_band_tiles(rpb):
    heads, n_dr, n_dc = rpb.shape
    win_cols = min(NA_WIN_COLS, GRID_W)
    lead = GRID_W - NA_WIN_COLS
    padded = jnp.pad(rpb.astype(F32) * LOG2_E, ((0, 0), (0, 0), (lead, 2 * GRID_W - 1 - lead - n_dc)))
    band = jnp.stack([padded[:, :, GRID_W - 1 - qc:2 * GRID_W - 1 - qc] for qc in range(GRID_W)], axis=2)
    col = np.arange(GRID_W)
    c0 = np.clip(col - win_cols // 2, 0, GRID_W - win_cols)
    valid_c = (col[None, :] >= c0[:, None]) & (col[None, :] < c0[:, None] + win_cols)
    band = jnp.where(jnp.asarray(valid_c)[None, None], band, NEG_BIG)
    masked = jnp.full((heads, 1, GRID_W, GRID_W), NEG_BIG, F32)
    return jnp.concatenate([band, masked], axis=1)


def _attention(q, k, v, kc, vc, band):
    bsz, n_tok, d = q.shape
    n_ctx = kc.shape[1]
    n_rows = n_tok // GRID_W
    n_blocks = n_rows // ATT_ROWS
    rq = ATT_ROWS * GRID_W
    nk = (ATT_ROWS + NA_WIN_ROWS - 1) * GRID_W
    pair = ATT_PAIRS * LANES
    q_spec = pl.BlockSpec((None, rq, pair), lambda b, hp, rb: (b, rb, hp))
    seq_spec = pl.BlockSpec((None, n_tok, pair), lambda b, hp, rb: (b, 0, hp))
    ctx_spec = pl.BlockSpec((None, n_ctx, pair), lambda b, hp, rb: (b, 0, hp))
    band_spec = pl.BlockSpec((2 * ATT_PAIRS,) + band.shape[1:], lambda b, hp, rb: (hp, 0, 0, 0))
    return pl.pallas_call(
        functools.partial(_attn_kernel, n_rows=n_rows, plan=_attn_tile_plan(n_rows)),
        grid=(bsz, d // pair, n_blocks),
        in_specs=[q_spec, seq_spec, seq_spec, ctx_spec, ctx_spec, band_spec],
        out_specs=q_spec,
        out_shape=jax.ShapeDtypeStruct((bsz, n_tok, d), BF16),
        scratch_shapes=[pltpu.VMEM((ATT_PAIRS, 2, rq, nk), F32)],
        compiler_params=_cparams(("arbitrary", "arbitrary", "arbitrary")),
        name="nbr_attention",
    )(q, k, v, kc, vc, band)


def _ctx_attn_kernel(q_ref, k_ref, v_ref, o_ref):
    q = q_ref[...]
    kv_blocks = [(k_ref[...], v_ref[...])]
    (o,) = _attend([(q, hd, kv_blocks, None) for hd in range(2)])
    o_ref[...] = o.astype(o_ref.dtype)


def _ctx_attention(qc, kc, vc):
    bsz, n_ctx, d = qc.shape
    pair = 2 * (d // NA_HEADS)
    spec = pl.BlockSpec((None, n_ctx, pair), lambda b, hp: (b, 0, hp))
    return pl.pallas_call(
        _ctx_attn_kernel,
        grid=(bsz, d // pair),
        in_specs=[spec, spec, spec],
        out_specs=spec,
        out_shape=jax.ShapeDtypeStruct((bsz, n_ctx, d), BF16),
        compiler_params=_cparams(("arbitrary", "arbitrary")),
        name="ctx_attention",
    )(qc, kc, vc)


def _gelu_tanh(x):
    c = float(np.sqrt(2.0 / np.pi))
    return 0.5 * x * (1.0 + jnp.tanh(c * (x + 0.044715 * (x * x * x))))


def _lru_in_kernel(x_ref, xp_ref, xn_ref, mod_ref, npre_ref, wg_ref, wr_ref, cw_ref, cb_ref,
                   u_ref, g_ref, rec_s):
    tm = x_ref.shape[0]
    halo = xp_ref.shape[0]
    i = pl.program_id(1)
    last = pl.num_programs(1) - 1
    mod = mod_ref[...]
    npre = npre_ref[...]
    x_ext = jnp.concatenate([xp_ref[...], x_ref[...], xn_ref[...]], axis=0)
    h_ext = _modulate(x_ext, npre, mod, 1)
    rec = jnp.dot(h_ext.astype(BF16), wr_ref[...], preferred_element_type=F32)
    gate = jnp.dot(h_ext[halo:halo + tm].astype(BF16), wg_ref[...], preferred_element_type=F32)
    row = lax.broadcasted_iota(jnp.int32, (tm + 2 * halo, 1), 0)
    valid = ((row >= halo) | (i > 0)) & ((row < halo + tm) | (i < last))
    rec = jnp.where(valid, rec, 0.0)
    cw = cw_ref[...]
    cb = cb_ref[...]
    n_blk, _, blk_w = u_ref.shape
    seg = tm // SUBLANES
    for n in range(n_blk):
        cols = slice(n * blk_w, (n + 1) * blk_w)
        rec_s[n] = rec[:, cols]
        u = cb[:, cols]
        for tap in range(CONV_W):
            u = u + rec_s[n, pl.ds(halo - CONV_PAD_LEFT + tap, tm), :] * cw[tap:tap + 1, cols]
        for j in range(SUBLANES):
            u_ref[n, pl.ds(j, seg, stride=SUBLANES), :] = u[j * seg:(j + 1) * seg]
    g_ref[...] = _gelu_tanh(gate).astype(g_ref.dtype)


def _lru_in(x3d, mod, layer, norm_pre, w_in, conv_w, conv_b, w_layer):
    bsz, n_tok, d = x3d.shape
    width = w_in.shape[2] // 2
    tm = min(SCAN_ROWS, n_tok)
    n_blk = width // LANES
    halo = SUBLANES
    halo_blocks = n_tok // halo
    per_tile = tm // halo
    out_spec = pl.BlockSpec((None, tm, width), lambda b, i: (b, i, 0))
    u_spec = pl.BlockSpec((None, None, n_blk, tm, LANES), lambda b, i: (b, i, 0, 0, 0))
    return pl.pallas_call(
        _lru_in_kernel,
        grid=(bsz, n_tok // tm),
        in_specs=[
            pl.BlockSpec((None, tm, d), lambda b, i: (b, i, 0)),
            pl.BlockSpec((None, halo, d), lambda b, i: (b, jnp.maximum(i * per_tile - 1, 0), 0)),
            pl.BlockSpec((None, halo, d),
                         lambda b, i: (b, jnp.minimum((i + 1) * per_tile, halo_blocks - 1), 0)),
            _mod_spec(mod, lambda b, i: b),
            _layer_spec(norm_pre, (layer,)),
            _layer_spec(w_in, (w_layer,), cols=(width, 0)), _layer_spec(w_in, (w_layer,), cols=(width, 1)),
            _layer_spec(conv_w, (w_layer,)), _layer_spec(conv_b, (w_layer,)),
        ],
        out_specs=[u_spec, out_spec],
        out_shape=[jax.ShapeDtypeStruct((bsz, n_tok // tm, n_blk, tm, LANES), F32),
                   jax.ShapeDtypeStruct((bsz, n_tok, width), BF16)],
        scratch_shapes=[pltpu.VMEM((n_blk, tm + 2 * halo, LANES), F32)],
        compiler_params=_cparams(("arbitrary", "arbitrary")),
        name="lru_in",
    )(x3d, x3d, x3d, mod[0], norm_pre, w_in, w_in, conv_w, conv_b)


def _scan_kernel(*refs, reverse, combine):
    if combine:
        (u_ref, wcat_ref, ba_ref, bx_ref, lam_ref, init_ref, prev_ref, g_ref,
         o_ref, fin_ref, h_s, p_s, carry_s) = refs
    else:
        (u_ref, wcat_ref, ba_ref, bx_ref, lam_ref, init_ref,
         o_ref, fin_ref, h_s, p_s, carry_s) = refs
    n_blk, tt, blk_w = u_ref.shape
    seg = tt // SUBLANES

    @pl.when(pl.program_id(1) == 0)
    def _():
        carry_s[...] = init_ref[...]

    lam = lam_ref[...]
    softplus = jnp.maximum(-lam, 0.0) + jnp.log1p(jnp.exp(-jnp.abs(lam)))
    half_log2_decay = (-0.5 * LRU_C * LOG2_E) * softplus
    segments = range(SUBLANES - 1, -1, -1) if reverse else range(SUBLANES)

    for n in range(n_blk):
        cols = slice(n * blk_w, (n + 1) * blk_w)
        un = u_ref[n]
        z = jnp.dot(un.astype(BF16), wcat_ref[n], preferred_element_type=F32)
        t_r = jnp.tanh(z[:, :blk_w] + ba_ref[:, cols])
        t_i = jnp.tanh(z[:, blk_w:] + bx_ref[:, cols])
        k = half_log2_decay[:, cols]
        a = jnp.exp2(k + k * t_r)
        one_m = 1.0 - a * a
        p_s[n] = a
        h_s[n] = (one_m * lax.rsqrt(jnp.maximum(one_m, SQRT_TINY))) * ((0.5 + 0.5 * t_i) * un)

    def scan_step(t, carry):
        i = (seg - 1 - t) if reverse else t
        tile = pl.ds(pl.multiple_of(i * SUBLANES, SUBLANES), SUBLANES)
        hs, ps = carry
        new_h, new_p = [], []
        for n in range(n_blk):
            a_t = p_s[n, tile, :]
            h = a_t * hs[n] + h_s[n, tile, :]
            p = a_t * ps[n]
            h_s[n, tile, :] = h
            p_s[n, tile, :] = p
            new_h.append(h)
            new_p.append(p)
        return tuple(new_h), tuple(new_p)

    zeros = tuple(jnp.zeros((SUBLANES, blk_w), F32) for _ in range(n_blk))
    ones = tuple(jnp.ones((SUBLANES, blk_w), F32) for _ in range(n_blk))
    h_end, p_end = lax.fori_loop(0, seg, scan_step, (zeros, ones), unroll=4)

    for n in range(n_blk):
        cols = slice(n * blk_w, (n + 1) * blk_w)
        h, p = h_end[n], p_end[n]
        state = carry_s[:, cols]
        entry = [None] * SUBLANES
        for j in segments:
            entry[j] = state
            state = h[j:j + 1] + p[j:j + 1] * state
        carry_s[:, cols] = state
        for j in range(SUBLANES):
            rows = slice(j * seg, (j + 1) * seg)
            picked = pl.ds(j, seg, stride=SUBLANES)
            ht = h_s[n, picked, :] + p_s[n, picked, :] * entry[j]
            if combine:
                ht = (ht + prev_ref[rows, cols]) * g_ref[rows, cols].astype(F32)
            o_ref[rows, cols] = ht.astype(o_ref.dtype)

    fin_ref[...] = carry_s[...]


def _scan(u, width, wcat, b_a, b_x, lam, init, reverse, prev=None, gate=None):
    bsz, n_chunks, n_blk, tt, blk_w = u.shape
    n_tok = n_chunks * tt
    combine = prev is not None
    if reverse:
        chunk = lambda i: n_chunks - 1 - i
    else:
        chunk = lambda i: i
    u_spec = pl.BlockSpec((None, None, n_blk, tt, blk_w), lambda b, i: (b, chunk(i), 0, 0, 0))
    seq_spec = pl.BlockSpec((None, tt, width), lambda b, i: (b, chunk(i), 0))
    vec_spec = pl.BlockSpec((None, 1, width), lambda b, i: (b, 0, 0))
    in_specs = [u_spec, _const_spec(wcat.shape), _const_spec((1, width)), _const_spec((1, width)),
                _const_spec((1, width)), vec_spec]
    args = [u, wcat, b_a.reshape(1, width), b_x.reshape(1, width), lam.reshape(1, width), init]
    if combine:
        in_specs += [seq_spec, seq_spec]
        args += [prev, gate]
    return pl.pallas_call(
        functools.partial(_scan_kernel, reverse=reverse, combine=combine),
        grid=(bsz, n_chunks),
        in_specs=in_specs,
        out_specs=[seq_spec, vec_spec],
        out_shape=[jax.ShapeDtypeStruct((bsz, n_tok, width), BF16 if combine else F32),
                   jax.ShapeDtypeStruct((bsz, 1, width), F32)],
        scratch_shapes=[pltpu.VMEM((n_blk, tt, blk_w), F32), pltpu.VMEM((n_blk, tt, blk_w), F32),
                        pltpu.VMEM((1, width), F32)],
        compiler_params=_cparams(("arbitrary", "arbitrary")),
        name="lru_scan_bwd" if reverse else "lru_scan_fwd",
    )(*args)


def _bidirectional_lru(u, gate, wcat, b_a, b_x, lam, init_f, init_b):
    width = gate.shape[-1]
    h_f, fin_f = _scan(u, width, wcat[0], b_a[0], b_x[0], lam[0], init_f, reverse=False)
    act, fin_b = _scan(u, width, wcat[1], b_a[1], b_x[1], lam[1], init_b, reverse=True,
                       prev=h_f, gate=gate)
    return act, fin_f, fin_b


def _rope_tables(n_tok, head_dim):
    pairs = head_dim // 4
    t = jnp.arange(n_tok, dtype=jnp.int32)
    row = (t // GRID_W).astype(F32)
    col = (t % GRID_W).astype(F32)
    inv_freq = jnp.power(ROPE_BASE, -jnp.arange(pairs, dtype=F32) / pairs)
    ang = jnp.concatenate([row[:, None] * inv_freq, col[:, None] * inv_freq], axis=-1)
    cos, sin = jnp.cos(ang), jnp.sin(ang)
    cos_h = jnp.concatenate([cos, cos], axis=-1)
    sin_h = jnp.concatenate([-sin, sin], axis=-1)
    return jnp.tile(cos_h, (1, 2)), jnp.tile(sin_h, (1, 2))


def kernel(x, c, ctx, c_ctx, ada_w, ada_b, norm_pre, norm_post, ffn_w_gate, ffn_w_up, ffn_w_down,
           na_w_qkv, na_w_o, na_rpb, lru_w_in, lru_conv_w, lru_conv_b, lru_w_a, lru_b_a, lru_w_x,
           lru_b_x, lru_lambda, lru_w_o):
    bsz, n_tok, d = x.shape
    n_ctx = ctx.shape[1]
    depth = ada_w.shape[0]
    n_mod = ada_w.shape[2] // d
    head_dim = d // NA_HEADS
    lru_width = lru_w_o.shape[1]
    n_mixers = 2

    cond = jnp.concatenate([c, c_ctx[None, :], jnp.zeros((SUBLANES - bsz - 1, d), F32)], axis=0)
    mods = _adaln(cond, ada_w, ada_b).reshape(depth, SUBLANES, n_mod, d)

    cos_t, sin_t = _rope_tables(n_tok, head_dim)
    ones_t = jnp.ones((n_ctx, cos_t.shape[1]), F32)
    zeros_t = jnp.zeros((n_ctx, cos_t.shape[1]), F32)

    wg_all, wu_all, wd_all = (w.astype(BF16) for w in (ffn_w_gate, ffn_w_up, ffn_w_down))
    w_qkv_all, na_wo_all = na_w_qkv.astype(BF16), na_w_o.astype(BF16)
    w_in_all, lru_wo_all = lru_w_in.astype(BF16), lru_w_o.astype(BF16)
    conv_b_all = lru_conv_b.reshape(lru_conv_b.shape[0], 1, lru_width)

    xl = x.reshape(bsz * n_tok, d)
    xc = ctx.reshape(bsz * n_ctx, d)
    for i in range(depth):
        last = i == depth - 1
        j = i // n_mixers
        mod = (mods, i, 0, True)
        mod_c = (mods, i, bsz, False)

        def half_ffn(u2d, rows, m, s, mix=None):
            return _ffn(u2d, rows, m, i, norm_pre, norm_post, wg_all, wu_all, wd_all, s, mix)

        xl = half_ffn(xl, n_tok, mod, 0)
        xc = half_ffn(xc, n_ctx, mod_c, 0)
        if i % n_mixers == 0:
            wo_all = na_wo_all
            q, k, v = _qkv(xl, n_tok, mod, i, norm_pre, cos_t, sin_t, w_qkv_all, j)
            qc, kc, vc = _qkv(xc, n_ctx, mod_c, i, norm_pre, ones_t, zeros_t, w_qkv_all, j)
            shp, shp_c = (bsz, n_tok, d), (bsz, n_ctx, d)
            kc3, vc3 = kc.reshape(shp_c), vc.reshape(shp_c)
            band = _attn_band_tiles(na_rpb[j])
            act = _attention(q.reshape(shp), k.reshape(shp), v.reshape(shp), kc3, vc3, band)
            act = act.reshape(bsz * n_tok, d)
            if not last:
                act_c = _ctx_attention(qc.reshape(shp_c), kc3, vc3).reshape(bsz * n_ctx, d)
        else:
            wo_all = lru_wo_all
            wcat = (0.5 * jnp.concatenate([lru_w_a[j], lru_w_x[j]], axis=-1)).astype(BF16)
            half_b_a, half_b_x = 0.5 * lru_b_a[j], 0.5 * lru_b_x[j]
            u_l, g_l = _lru_in(xl.reshape(bsz, n_tok, d), mod, i, norm_pre, w_in_all,
                               lru_conv_w, conv_b_all, j)
            u_c, g_c = _lru_in(xc.reshape(bsz, n_ctx, d), mod_c, i, norm_pre, w_in_all,
                               lru_conv_w, conv_b_all, j)
            zero_state = jnp.zeros((bsz, 1, lru_width), F32)
            act_c, fin_f, fin_b = _bidirectional_lru(u_c, g_c, wcat, half_b_a, half_b_x,
                                                     lru_lambda[j], zero_state, zero_state)
            act, _, _ = _bidirectional_lru(u_l, g_l, wcat, half_b_a, half_b_x,
                                           lru_lambda[j], fin_f, fin_b)
            act = act.reshape(bsz * n_tok, lru_width)
            act_c = act_c.reshape(bsz * n_ctx, lru_width)
        xl = half_ffn(xl, n_tok, mod, 2, mix=(act, wo_all, j))
        if not last:
            xc = half_ffn(xc, n_ctx, mod_c, 2, mix=(act_c, wo_all, j))
    return xl.reshape(bsz, n_tok, d)
```

```python
import functools

import numpy as np
import jax
import jax.numpy as jnp
from jax import lax
from jax.experimental import pallas as pl
from jax.experimental.pallas import tpu as pltpu

F32 = jnp.float32
BF16 = jnp.bfloat16

GRID_W = 64
NA_HEADS = 16
NA_WIN_ROWS = 8
NA_WIN_COLS = 16
ROPE_BASE = 10000.0
LRU_C = 8.0
CONV_W = 4
CONV_PAD_LEFT = 2
RMS_EPS = 1e-6
FFN_RES_WEIGHT = 0.5

LANES = 128
SUBLANES = 8
MXU_DIM = 256

FFN_ROWS = 1024
FFN_SUB_ROWS = 512
FFN_HID_CHUNK = MXU_DIM
PROJ_ROWS = 1024
PROJ_SUB_ROWS = 512
ATT_ROWS = 4
ATT_PAIRS = 4
SCAN_ROWS = 1024
MOD_COLS = 1024
VMEM_LIMIT = 56 * 1024 * 1024
NEG_BIG = -1e30
SQRT_TINY = 1e-30
LOG2_E = float(np.log2(np.e))


def _cparams(sem):
    return pltpu.CompilerParams(dimension_semantics=sem, vmem_limit_bytes=VMEM_LIMIT)


def _const_spec(shape):
    nd = len(shape)
    return pl.BlockSpec(shape, lambda *_: (0,) * nd, pipeline_mode=pl.Buffered(1))


def _layer_spec(arr, lead, cols=None):
    tail = arr.shape[len(lead):]
    if cols is not None:
        tail = tail[:-1] + (cols[0],)
    idx = tuple(lead) + (0,) * (len(tail) - 1) + (0 if cols is None else cols[1],)
    return pl.BlockSpec((None,) * len(lead) + tail, lambda *_: idx, pipeline_mode=pl.Buffered(1))


def _mod_spec(mod, batch_of):
    table, layer, row, per_batch = mod
    if per_batch:
        index_map = lambda *g: (layer, row + batch_of(*g), 0, 0)
    else:
        index_map = lambda *g: (layer, row, 0, 0)
    return pl.BlockSpec((None, None) + table.shape[2:], index_map)


def _rms(x, gain):
    ms = jnp.mean(x * x, axis=-1, keepdims=True)
    return x * lax.rsqrt(ms + RMS_EPS) * gain


def _modulate(x, npre, mod, s):
    return _rms(x, npre[s:s + 1]) * (1.0 + mod[3 * s + 1:3 * s + 2]) + mod[3 * s:3 * s + 1]


def _mod_kernel(c_ref, w_ref, b_ref, o_ref):
    c = c_ref[...]
    s = (c * jax.nn.sigmoid(c)).astype(BF16)
    o_ref[...] = jnp.dot(s, w_ref[...].astype(BF16), preferred_element_type=F32) + b_ref[...]


def _adaln(cond, ada_w, ada_b):
    depth, d, n = ada_w.shape
    rows = cond.shape[0]
    return pl.pallas_call(
        _mod_kernel,
        grid=(depth, n // MOD_COLS),
        in_specs=[
            pl.BlockSpec((rows, d), lambda l, j: (0, 0)),
            pl.BlockSpec((None, d, MOD_COLS), lambda l, j: (l, 0, j)),
            pl.BlockSpec((None, 1, MOD_COLS), lambda l, j: (l, 0, j)),
        ],
        out_specs=pl.BlockSpec((None, rows, MOD_COLS), lambda l, j: (l, 0, j)),
        out_shape=jax.ShapeDtypeStruct((depth, rows, n), F32),
        compiler_params=_cparams(("arbitrary", "arbitrary")),
        name="adaln_mod",
    )(cond, ada_w, ada_b.reshape(depth, 1, n))


def _ffn_kernel(*refs, s, has_mix):
    if has_mix:
        x_ref, a_ref, wo_ref, mod_ref, npre_ref, npost_ref, wg_ref, wu_ref, wd_ref, o_ref = refs
    else:
        x_ref, mod_ref, npre_ref, npost_ref, wg_ref, wu_ref, wd_ref, o_ref = refs
    mod = mod_ref[...]
    npre = npre_ref[...]
    npost = npost_ref[...]
    hid = wg_ref.shape[1]
    sub = min(FFN_SUB_ROWS, x_ref.shape[0])
    blocks = [slice(r, r + sub) for r in range(0, x_ref.shape[0], sub)]

    def prologue(rows):
        x = x_ref[rows, :]
        if has_mix:
            y = jnp.dot(a_ref[rows, :], wo_ref[...], preferred_element_type=F32)
            x = x + mod[5:6] * _rms(y, npost[1:2])
        return x, _modulate(x, npre, mod, s).astype(BF16)

    def swiglu(rows, x, h):
        acts = []
        for c in range(0, hid, FFN_HID_CHUNK):
            g = jnp.dot(h, wg_ref[:, c:c + FFN_HID_CHUNK], preferred_element_type=F32)
            u = jnp.dot(h, wu_ref[:, c:c + FFN_HID_CHUNK], preferred_element_type=F32)
            acts.append((g * jax.nn.sigmoid(g) * u).astype(BF16))
        acc = jnp.dot(jnp.concatenate(acts, axis=1), wd_ref[...], preferred_element_type=F32)
        o_ref[rows, :] = x + FFN_RES_WEIGHT * mod[3 * s + 2:3 * s + 3] * _rms(acc, npost[s:s + 1])

    ready = prologue(blocks[0])
    for k, rows in enumerate(blocks):
        upcoming = prologue(blocks[k + 1]) if k + 1 < len(blocks) else None
        swiglu(rows, *ready)
        ready = upcoming


def _ffn(x2d, rows_per_batch, mod, layer, norm_pre, norm_post, wg, wu, wd, s, mix=None):
    n, d = x2d.shape
    tm = min(FFN_ROWS, n)
    half = (layer, s // 2)
    row_spec = pl.BlockSpec((tm, d), lambda i: (i, 0))
    in_specs = [row_spec]
    args = [x2d]
    if mix is not None:
        a2d, wo, wo_layer = mix
        in_specs += [pl.BlockSpec((tm, a2d.shape[1]), lambda i: (i, 0)), _layer_spec(wo, (wo_layer,))]
        args += [a2d, wo]
    in_specs += [_mod_spec(mod, lambda i: (i * tm) // rows_per_batch),
                 _layer_spec(norm_pre, (layer,)), _layer_spec(norm_post, (layer,)),
                 _layer_spec(wg, half), _layer_spec(wu, half), _layer_spec(wd, half)]
    args += [mod[0], norm_pre, norm_post, wg, wu, wd]
    return pl.pallas_call(
        functools.partial(_ffn_kernel, s=s, has_mix=mix is not None),
        grid=(n // tm,),
        in_specs=in_specs,
        out_specs=row_spec,
        out_shape=jax.ShapeDtypeStruct((n, d), F32),
        compiler_params=_cparams(("arbitrary",)),
        name="mix_ffn" if mix is not None else "ffn",
    )(*args)


def _qkv_kernel(x_ref, mod_ref, npre_ref, cos_ref, sin_ref, w_ref, q_ref, k_ref, v_ref, *, scale):
    d = x_ref.shape[1]
    half = d // NA_HEADS // 2
    reps = d // cos_ref.shape[1]
    mod = mod_ref[...]
    npre = npre_ref[...]
    sub = min(PROJ_SUB_ROWS, x_ref.shape[0])
    blocks = [slice(r, r + sub) for r in range(0, x_ref.shape[0], sub)]
    lane = lax.broadcasted_iota(jnp.int32, (sub, d), 1)
    first = (lane & (2 * half - 1)) < half

    def rope(t, rows):
        cosf = jnp.tile(cos_ref[rows, :], (1, reps))
        sinf = jnp.tile(sin_ref[rows, :], (1, reps))
        partner = jnp.where(first, pltpu.roll(t, d - half, 1), pltpu.roll(t, half, 1))
        return t * cosf + partner * sinf

    hs = [_modulate(x_ref[rows, :], npre, mod, 1).astype(BF16) for rows in blocks]
    for rows, h in zip(blocks, hs):
        q = jnp.dot(h, w_ref[:, 0:d], preferred_element_type=F32)
        q_ref[rows, :] = (rope(q, rows) * scale).astype(BF16)
        k = jnp.dot(h, w_ref[:, d:2 * d], preferred_element_type=F32)
        k_ref[rows, :] = rope(k, rows).astype(BF16)
        v = jnp.dot(h, w_ref[:, 2 * d:3 * d], preferred_element_type=F32)
        v_ref[rows, :] = v.astype(BF16)


def _qkv(x2d, rows_per_batch, mod, layer, norm_pre, cos_t, sin_t, w_qkv, w_layer):
    n, d = x2d.shape
    tm = min(PROJ_ROWS, rows_per_batch)
    tiles_per_batch = rows_per_batch // tm
    row_spec = pl.BlockSpec((tm, d), lambda i: (i, 0))
    tab_spec = pl.BlockSpec((tm, cos_t.shape[1]), lambda i: (i % tiles_per_batch, 0))
    out = jax.ShapeDtypeStruct((n, d), BF16)
    return pl.pallas_call(
        functools.partial(_qkv_kernel, scale=float((d // NA_HEADS) ** -0.5) * LOG2_E),
        grid=(n // tm,),
        in_specs=[row_spec, _mod_spec(mod, lambda i: i // tiles_per_batch),
                  _layer_spec(norm_pre, (layer,)), tab_spec, tab_spec, _layer_spec(w_qkv, (w_layer,))],
        out_specs=[row_spec, row_spec, row_spec],
        out_shape=[out, out, out],
        compiler_params=_cparams(("arbitrary",)),
        name="qkv_rope",
    )(x2d, mod[0], norm_pre, cos_t, sin_t, w_qkv)


_NT = (((1,), (1,)), ((), ()))


def _lane_is_head0(shape):
    return lax.broadcasted_iota(jnp.int32, shape, len(shape) - 1) < shape[-1] // 2


def _head_scores(q, hd, kv_blocks, bias):
    is0 = _lane_is_head0(q.shape)
    qh = jnp.where(is0 if hd == 0 else jnp.logical_not(is0), q, jnp.zeros_like(q))
    scores = []
    for blk, (k, _) in enumerate(kv_blocks):
        s = lax.dot_general(qh, k, _NT, preferred_element_type=F32)
        if blk == 0 and bias is not None:
            s = s + bias
        scores.append(s)
    return scores


def _head_output(scores, hd, kv_blocks):
    m = functools.reduce(jnp.maximum, [jnp.max(s, axis=-1, keepdims=True) for s in scores])
    o = None
    for s, (_, v) in zip(scores, kv_blocks):
        is0 = _lane_is_head0(v.shape)
        v_aug = jnp.where(is0 if hd == 0 else jnp.logical_not(is0), v, jnp.ones_like(v))
        part = jnp.dot(jnp.exp2(s - m).astype(BF16), v_aug, preferred_element_type=F32)
        o = part if o is None else o + part
    return o


def _attend(jobs):
    raw = []
    pending = None
    for q, hd, kv_blocks, bias in jobs:
        scores = _head_scores(q, hd, kv_blocks, bias)
        if pending is not None:
            raw.append(_head_output(*pending))
        pending = (scores, hd, kv_blocks)
    raw.append(_head_output(*pending))
    outs = []
    for o0, o1 in zip(raw[0::2], raw[1::2]):
        is0 = _lane_is_head0(o0.shape)
        num = jnp.where(is0, o0, o1)
        den = pltpu.roll(jnp.where(is0, o1, o0), o0.shape[-1] // 2, 1)
        outs.append(num / den)
    return outs


def _attn_tile_plan(n_rows):
    win_rows = min(NA_WIN_ROWS, n_rows)
    key_rows = ATT_ROWS + NA_WIN_ROWS - 1
    n_blocks = n_rows // ATT_ROWS
    masked = 2 * NA_WIN_ROWS - 1
    plan = np.full((3, ATT_ROWS, key_rows), masked, np.int64)
    for kind, blk in enumerate((0, 1, n_blocks - 1)):
        r_b = blk * ATT_ROWS
        w0 = int(np.clip(r_b - NA_WIN_ROWS // 2, 0, n_rows - key_rows))
        for ri in range(ATT_ROWS):
            r = r_b + ri
            r0 = int(np.clip(r - win_rows // 2, 0, n_rows - win_rows))
            for kri in range(key_rows):
                kr = w0 + kri
                if r0 <= kr < r0 + win_rows:
                    plan[kind, ri, kri] = kr - r + NA_WIN_ROWS - 1
    return plan


def _attn_kernel(q_ref, k_ref, v_ref, kc_ref, vc_ref, band_ref, o_ref, bias_s, *, n_rows, plan):
    nk = bias_s.shape[-1]
    rb = pl.program_id(2)
    n_blocks = pl.num_programs(2)

    def build(kind):
        for h in range(band_ref.shape[0]):
            for ri in range(ATT_ROWS):
                tiles = [band_ref[h, int(e)] for e in plan[kind, ri]]
                bias_s[h // 2, h % 2, ri * GRID_W:(ri + 1) * GRID_W, :] = jnp.concatenate(tiles, axis=1)

    for kind, blk in enumerate((0, 1, n_blocks - 1)):
        pl.when(rb == blk)(functools.partial(build, kind))

    w0 = jnp.clip(rb * ATT_ROWS - NA_WIN_ROWS // 2, 0, n_rows - nk // GRID_W)
    start = pl.multiple_of(w0 * GRID_W, GRID_W)
    jobs = []
    for pp in range(bias_s.shape[0]):
        lanes = slice(pp * LANES, (pp + 1) * LANES)
        kv_blocks = [(k_ref[pl.ds(start, nk), lanes], v_ref[pl.ds(start, nk), lanes]),
                     (kc_ref[:, lanes], vc_ref[:, lanes])]
        q = q_ref[:, lanes]
        jobs += [(q, hd, kv_blocks, bias_s[pp, hd]) for hd in range(2)]
    for pp, o in enumerate(_attend(jobs)):
        o_ref[:, pp * LANES:(pp + 1) * LANES] = o.astype(o_ref.dtype)


def _attn_band_tiles(rpb):
    heads, n_dr, n_dc = rpb.shape
    win_cols = min(NA_WIN_COLS, GRID_W)
    lead = GRID_W - NA_WIN_COLS
    padded = jnp.pad(rpb.astype(F32) * LOG2_E, ((0, 0), (0, 0), (lead, 2 * GRID_W - 1 - lead - n_dc)))
    band = jnp.stack([padded[:, :, GRID_W - 1 - qc:2 * GRID_W - 1 - qc] for qc in range(GRID_W)], axis=2)
    col = np.arange(GRID_W)
    c0 = np.clip(col - win_cols // 2, 0, GRID_W - win_cols)
    valid_c = (col[None, :] >= c0[:, None]) & (col[None, :] < c0[:, None] + win_cols)
    band = jnp.where(jnp.asarray(valid_c)[None, None], band, NEG_BIG)
    masked = jnp.full((heads, 1, GRID_W, GRID_W), NEG_BIG, F32)
    return jnp.concatenate([band, masked], axis=1)


def _attention(q, k, v, kc, vc, band):
    bsz, n_tok, d = q.shape
    n_ctx = kc.shape[1]
    n_rows = n_tok // GRID_W
    n_blocks = n_rows // ATT_ROWS
    rq = ATT_ROWS * GRID_W
    nk = (ATT_ROWS + NA_WIN_ROWS - 1) * GRID_W
    pair = ATT_PAIRS * LANES
    q_spec = pl.BlockSpec((None, rq, pair), lambda b, hp, rb: (b, rb, hp))
    seq_spec = pl.BlockSpec((None, n_tok, pair), lambda b, hp, rb: (b, 0, hp),
                            pipeline_mode=pl.Buffered(1))
    ctx_spec = pl.BlockSpec((None, n_ctx, pair), lambda b, hp, rb: (b, 0, hp))
    band_spec = pl.BlockSpec((2 * ATT_PAIRS,) + band.shape[1:], lambda b, hp, rb: (hp, 0, 0, 0))
    return pl.pallas_call(
        functools.partial(_attn_kernel, n_rows=n_rows, plan=_attn_tile_plan(n_rows)),
        grid=(bsz, d // pair, n_blocks),
        in_specs=[q_spec, seq_spec, seq_spec, ctx_spec, ctx_spec, band_spec],
        out_specs=q_spec,
        out_shape=jax.ShapeDtypeStruct((bsz, n_tok, d), BF16),
        scratch_shapes=[pltpu.VMEM((ATT_PAIRS, 2, rq, nk), F32)],
        compiler_params=_cparams(("arbitrary", "arbitrary", "arbitrary")),
        name="nbr_attention",
    )(q, k, v, kc, vc, band)


def _ctx_attn_kernel(q_ref, k_ref, v_ref, o_ref):
    q = q_ref[...]
    kv_blocks = [(k_ref[...], v_ref[...])]
    (o,) = _attend([(q, hd, kv_blocks, None) for hd in range(2)])
    o_ref[...] = o.astype(o_ref.dtype)


def _ctx_attention(qc, kc, vc):
    bsz, n_ctx, d = qc.shape
    pair = 2 * (d // NA_HEADS)
    spec = pl.BlockSpec((None, n_ctx, pair), lambda b, hp: (b, 0, hp))
    return pl.pallas_call(
        _ctx_attn_kernel,
        grid=(bsz, d // pair),
        in_specs=[spec, spec, spec],
        out_specs=spec,
        out_shape=jax.ShapeDtypeStruct((bsz, n_ctx, d), BF16),
        compiler_params=_cparams(("arbitrary", "arbitrary")),
        name="ctx_attention",
    )(qc, kc, vc)


def _gelu_tanh(x):
    c = float(np.sqrt(2.0 / np.pi))
    return 0.5 * x * (1.0 + jnp.tanh(c * (x + 0.044715 * (x * x * x))))


def _lru_in_kernel(x_ref, xp_ref, xn_ref, mod_ref, npre_ref, wg_ref, wr_ref, cw_ref, cb_ref,
                   u_ref, g_ref, rec_s):
    tm = x_ref.shape[0]
    halo = xp_ref.shape[0]
    i = pl.program_id(1)
    last = pl.num_programs(1) - 1
    mod = mod_ref[...]
    npre = npre_ref[...]
    x_ext = jnp.concatenate([xp_ref[...], x_ref[...], xn_ref[...]], axis=0)
    h_ext = _modulate(x_ext, npre, mod, 1)
    rec = jnp.dot(h_ext.astype(BF16), wr_ref[...], preferred_element_type=F32)
    gate = jnp.dot(h_ext[halo:halo + tm].astype(BF16), wg_ref[...], preferred_element_type=F32)
    row = lax.broadcasted_iota(jnp.int32, (tm + 2 * halo, 1), 0)
    valid = ((row >= halo) | (i > 0)) & ((row < halo + tm) | (i < last))
    rec = jnp.where(valid, rec, 0.0)
    cw = cw_ref[...]
    cb = cb_ref[...]
    n_blk, _, blk_w = u_ref.shape
    seg = tm // SUBLANES
    for n in range(n_blk):
        cols = slice(n * blk_w, (n + 1) * blk_w)
        rec_s[n] = rec[:, cols]
        u = cb[:, cols]
        for tap in range(CONV_W):
            u = u + rec_s[n, pl.ds(halo - CONV_PAD_LEFT + tap, tm), :] * cw[tap:tap + 1, cols]
        for j in range(SUBLANES):
            u_ref[n, pl.ds(j, seg, stride=SUBLANES), :] = u[j * seg:(j + 1) * seg]
    g_ref[...] = _gelu_tanh(gate).astype(g_ref.dtype)


def _lru_in(x3d, mod, layer, norm_pre, w_in, conv_w, conv_b, w_layer):
    bsz, n_tok, d = x3d.shape
    width = w_in.shape[2] // 2
    tm = min(SCAN_ROWS, n_tok)
    n_blk = width // LANES
    halo = SUBLANES
    halo_blocks = n_tok // halo
    per_tile = tm // halo
    out_spec = pl.BlockSpec((None, tm, width), lambda b, i: (b, i, 0))
    u_spec = pl.BlockSpec((None, None, n_blk, tm, LANES), lambda b, i: (b, i, 0, 0, 0))
    return pl.pallas_call(
        _lru_in_kernel,
        grid=(bsz, n_tok // tm),
        in_specs=[
            pl.BlockSpec((None, tm, d), lambda b, i: (b, i, 0)),
            pl.BlockSpec((None, halo, d), lambda b, i: (b, jnp.maximum(i * per_tile - 1, 0), 0)),
            pl.BlockSpec((None, halo, d),
                         lambda b, i: (b, jnp.minimum((i + 1) * per_tile, halo_blocks - 1), 0)),
            _mod_spec(mod, lambda b, i: b),
            _layer_spec(norm_pre, (layer,)),
            _layer_spec(w_in, (w_layer,), cols=(width, 0)), _layer_spec(w_in, (w_layer,), cols=(width, 1)),
            _layer_spec(conv_w, (w_layer,)), _layer_spec(conv_b, (w_layer,)),
        ],
        out_specs=[u_spec, out_spec],
        out_shape=[jax.ShapeDtypeStruct((bsz, n_tok // tm, n_blk, tm, LANES), F32),
                   jax.ShapeDtypeStruct((bsz, n_tok, width), BF16)],
        scratch_shapes=[pltpu.VMEM((n_blk, tm + 2 * halo, LANES), F32)],
        compiler_params=_cparams(("arbitrary", "arbitrary")),
        name="lru_in",
    )(x3d, x3d, x3d, mod[0], norm_pre, w_in, w_in, conv_w, conv_b)


def _scan_kernel(*refs, reverse, combine):
    if combine:
        (u_ref, wcat_ref, ba_ref, bx_ref, lam_ref, init_ref, prev_ref, g_ref,
         o_ref, fin_ref, h_s, p_s, carry_s) = refs
    else:
        (u_ref, wcat_ref, ba_ref, bx_ref, lam_ref, init_ref,
         o_ref, fin_ref, h_s, p_s, carry_s) = refs
    n_blk, tt, blk_w = u_ref.shape
    seg = tt // SUBLANES

    @pl.when(pl.program_id(1) == 0)
    def _():
        carry_s[...] = init_ref[...]

    lam = lam_ref[...]
    softplus = jnp.maximum(-lam, 0.0) + jnp.log1p(jnp.exp(-jnp.abs(lam)))
    half_log2_decay = (-0.5 * LRU_C * LOG2_E) * softplus
    segments = range(SUBLANES - 1, -1, -1) if reverse else range(SUBLANES)

    for n in range(n_blk):
        cols = slice(n * blk_w, (n + 1) * blk_w)
        un = u_ref[n]
        z = jnp.dot(un.astype(BF16), wcat_ref[n], preferred_element_type=F32)
        t_r = jnp.tanh(z[:, :blk_w] + ba_ref[:, cols])
        t_i = jnp.tanh(z[:, blk_w:] + bx_ref[:, cols])
        k = half_log2_decay[:, cols]
        a = jnp.exp2(k + k * t_r)
        one_m = 1.0 - a * a
        p_s[n] = a
        h_s[n] = (one_m * lax.rsqrt(jnp.maximum(one_m, SQRT_TINY))) * ((0.5 + 0.5 * t_i) * un)

    def scan_step(t, carry):
        i = (seg - 1 - t) if reverse else t
        tile = pl.ds(pl.multiple_of(i * SUBLANES, SUBLANES), SUBLANES)
        hs, ps = carry
        new_h, new_p = [], []
        for n in range(n_blk):
            a_t = p_s[n, tile, :]
            h = a_t * hs[n] + h_s[n, tile, :]
            p = a_t * ps[n]
            h_s[n, tile, :] = h
            p_s[n, tile, :] = p
            new_h.append(h)
            new_p.append(p)
        return tuple(new_h), tuple(new_p)

    zeros = tuple(jnp.zeros((SUBLANES, blk_w), F32) for _ in range(n_blk))
    ones = tuple(jnp.ones((SUBLANES, blk_w), F32) for _ in range(n_blk))
    h_end, p_end = lax.fori_loop(0, seg, scan_step, (zeros, ones), unroll=4)

    for n in range(n_blk):
        cols = slice(n * blk_w, (n + 1) * blk_w)
        h, p = h_end[n], p_end[n]
        state = carry_s[:, cols]
        entry = [None] * SUBLANES
        for j in segments:
            entry[j] = state
            state = h[j:j + 1] + p[j:j + 1] * state
        carry_s[:, cols] = state
        for j in range(SUBLANES):
            rows = slice(j * seg, (j + 1) * seg)
            picked = pl.ds(j, seg, stride=SUBLANES)
            ht = h_s[n, picked, :] + p_s[n, picked, :] * entry[j]
            if combine:
                ht = (ht + prev_ref[rows, cols]) * g_ref[rows, cols].astype(F32)
            o_ref[rows, cols] = ht.astype(o_ref.dtype)

    fin_ref[...] = carry_s[...]


def _scan(u, width, wcat, b_a, b_x, lam, init, reverse, prev=None, gate=None):
    bsz, n_chunks, n_blk, tt, blk_w = u.shape
    n_tok = n_chunks * tt
    combine = prev is not None
    if reverse:
        chunk = lambda i: n_chunks - 1 - i
    else:
        chunk = lambda i: i
    u_spec = pl.BlockSpec((None, None, n_blk, tt, blk_w), lambda b, i: (b, chunk(i), 0, 0, 0))
    seq_spec = pl.BlockSpec((None, tt, width), lambda b, i: (b, chunk(i), 0))
    vec_spec = pl.BlockSpec((None, 1, width), lambda b, i: (b, 0, 0))
    in_specs = [u_spec, _const_spec(wcat.shape), _const_spec((1, width)), _const_spec((1, width)),
                _const_spec((1, width)), vec_spec]
    args = [u, wcat, b_a.reshape(1, width), b_x.reshape(1, width), lam.reshape(1, width), init]
    if combine:
        in_specs += [seq_spec, seq_spec]
        args += [prev, gate]
    return pl.pallas_call(
        functools.partial(_scan_kernel, reverse=reverse, combine=combine),
        grid=(bsz, n_chunks),
        in_specs=in_specs,
        out_specs=[seq_spec, vec_spec],
        out_shape=[jax.ShapeDtypeStruct((bsz, n_tok, width), BF16 if combine else F32),
                   jax.ShapeDtypeStruct((bsz, 1, width), F32)],
        scratch_shapes=[pltpu.VMEM((n_blk, tt, blk_w), F32), pltpu.VMEM((n_blk, tt, blk_w), F32),
                        pltpu.VMEM((1, width), F32)],
        compiler_params=_cparams(("arbitrary", "arbitrary")),
        name="lru_scan_bwd" if reverse else "lru_scan_fwd",
    )(*args)


def _bidirectional_lru(u, gate, wcat, b_a, b_x, lam, init_f, init_b):
    width = gate.shape[-1]
    h_f, fin_f = _scan(u, width, wcat[0], b_a[0], b_x[0], lam[0], init_f, reverse=False)
    act, fin_b = _scan(u, width, wcat[1], b_a[1], b_x[1], lam[1], init_b, reverse=True,
                       prev=h_f, gate=gate)
    return act, fin_f, fin_b


def _rope_tables(n_tok, head_dim):
    pairs = head_dim // 4
    t = jnp.arange(n_tok, dtype=jnp.int32)
    row = (t // GRID_W).astype(F32)
    col = (t % GRID_W).astype(F32)
    inv_freq = jnp.power(ROPE_BASE, -jnp.arange(pairs, dtype=F32) / pairs)
    ang = jnp.concatenate([row[:, None] * inv_freq, col[:, None] * inv_freq], axis=-1)
    cos, sin = jnp.cos(ang), jnp.sin(ang)
    cos_h = jnp.concatenate([cos, cos], axis=-1)
    sin_h = jnp.concatenate([-sin, sin], axis=-1)
    return jnp.tile(cos_h, (1, 2)), jnp.tile(sin_h, (1, 2))


def kernel(x, c, ctx, c_ctx, ada_w, ada_b, norm_pre, norm_post, ffn_w_gate, ffn_w_up, ffn_w_down,
           na_w_qkv, na_w_o, na_rpb, lru_w_in, lru_conv_w, lru_conv_b, lru_w_a, lru_b_a, lru_w_x,
           lru_b_x, lru_lambda, lru_w_o):
    bsz, n_tok, d = x.shape
    n_ctx = ctx.shape[1]
    depth = ada_w.shape[0]
    n_mod = ada_w.shape[2] // d
    head_dim = d // NA_HEADS
    lru_width = lru_w_o.shape[1]
    n_mixers = 2

    cond = jnp.concatenate([c, c_ctx[None, :], jnp.zeros((SUBLANES - bsz - 1, d), F32)], axis=0)
    mods = _adaln(cond, ada_w, ada_b).reshape(depth, SUBLANES, n_mod, d)

    cos_t, sin_t = _rope_tables(n_tok, head_dim)
    ones_t = jnp.ones((n_ctx, cos_t.shape[1]), F32)
    zeros_t = jnp.zeros((n_ctx, cos_t.shape[1]), F32)

    wg_all, wu_all, wd_all = (w.astype(BF16) for w in (ffn_w_gate, ffn_w_up, ffn_w_down))
    w_qkv_all, na_wo_all = na_w_qkv.astype(BF16), na_w_o.astype(BF16)
    w_in_all, lru_wo_all = lru_w_in.astype(BF16), lru_w_o.astype(BF16)
    conv_b_all = lru_conv_b.reshape(lru_conv_b.shape[0], 1, lru_width)

    xl = x.reshape(bsz * n_tok, d)
    xc = ctx.reshape(bsz * n_ctx, d)
    for i in range(depth):
        last = i == depth - 1
        j = i // n_mixers
        mod = (mods, i, 0, True)
        mod_c = (mods, i, bsz, False)

        def half_ffn(u2d, rows, m, s, mix=None):
            return _ffn(u2d, rows, m, i, norm_pre, norm_post, wg_all, wu_all, wd_all, s, mix)

        xl = half_ffn(xl, n_tok, mod, 0)
        xc = half_ffn(xc, n_ctx, mod_c, 0)
        if i % n_mixers == 0:
            wo_all = na_wo_all
            q, k, v = _qkv(xl, n_tok, mod, i, norm_pre, cos_t, sin_t, w_qkv_all, j)
            qc, kc, vc = _qkv(xc, n_ctx, mod_c, i, norm_pre, ones_t, zeros_t, w_qkv_all, j)
            shp, shp_c = (bsz, n_tok, d), (bsz, n_ctx, d)
            kc3, vc3 = kc.reshape(shp_c), vc.reshape(shp_c)
            band = _attn_band_tiles(na_rpb[j])
            act = _attention(q.reshape(shp), k.reshape(shp), v.reshape(shp), kc3, vc3, band)
            act = act.reshape(bsz * n_tok, d)
            if not last:
                act_c = _ctx_attention(qc.reshape(shp_c), kc3, vc3).reshape(bsz * n_ctx, d)
        else:
            wo_all = lru_wo_all
            wcat = (0.5 * jnp.concatenate([lru_w_a[j], lru_w_x[j]], axis=-1)).astype(BF16)
            half_b_a, half_b_x = 0.5 * lru_b_a[j], 0.5 * lru_b_x[j]
            u_l, g_l = _lru_in(xl.reshape(bsz, n_tok, d), mod, i, norm_pre, w_in_all,
                               lru_conv_w, conv_b_all, j)
            u_c, g_c = _lru_in(xc.reshape(bsz, n_ctx, d), mod_c, i, norm_pre, w_in_all,
                               lru_conv_w, conv_b_all, j)
            zero_state = jnp.zeros((bsz, 1, lru_width), F32)
            act_c, fin_f, fin_b = _bidirectional_lru(u_c, g_c, wcat, half_b_a, half_b_x,
                                                     lru_lambda[j], zero_state, zero_state)
            act, _, _ = _bidirectional_lru(u_l, g_l, wcat, half_b_a, half_b_x,
                                           lru_lambda[j], fin_f, fin_b)
            act = act.reshape(bsz * n_tok, lru_width)
            act_c = act_c.reshape(bsz * n_ctx, lru_width)
        xl = half_ffn(xl, n_tok, mod, 2, mix=(act, wo_all, j))
        if not last:
            xc = half_ffn(xc, n_ctx, mod_c, 2, mix=(act_c, wo_all, j))
    return xl.reshape(bsz, n_tok, d)
```

```python
import functools

import numpy as np
import jax
import jax.numpy as jnp
from jax import lax
from jax.experimental import pallas as pl
from jax.experimental.pallas import tpu as pltpu

F32 = jnp.float32
BF16 = jnp.bfloat16

GRID_W = 64
NA_HEADS = 16
NA_WIN_ROWS = 8
NA_WIN_COLS = 16
ROPE_BASE = 10000.0
LRU_C = 8.0
CONV_W = 4
CONV_PAD_LEFT = 2
RMS_EPS = 1e-6
FFN_RES_WEIGHT = 0.5

LANES = 128
SUBLANES = 8
MXU_DIM = 256

FFN_ROWS = 1024
FFN_SUB_ROWS = 512
FFN_HID_CHUNK = MXU_DIM
PROJ_ROWS = 1024
PROJ_SUB_ROWS = 512
ATT_ROWS = 4
ATT_PAIRS = 8
SCAN_ROWS = 1024
MOD_COLS = 1024
VMEM_LIMIT = 56 * 1024 * 1024
NEG_BIG = -1e30
SQRT_TINY = 1e-30
LOG2_E = float(np.log2(np.e))


def _cparams(sem):
    return pltpu.CompilerParams(dimension_semantics=sem, vmem_limit_bytes=VMEM_LIMIT)


def _const_spec(shape):
    nd = len(shape)
    return pl.BlockSpec(shape, lambda *_: (0,) * nd, pipeline_mode=pl.Buffered(1))


def _layer_spec(arr, lead, cols=None):
    tail = arr.shape[len(lead):]
    if cols is not None:
        tail = tail[:-1] + (cols[0],)
    idx = tuple(lead) + (0,) * (len(tail) - 1) + (0 if cols is None else cols[1],)
    return pl.BlockSpec((None,) * len(lead) + tail, lambda *_: idx, pipeline_mode=pl.Buffered(1))


def _mod_spec(mod, batch_of):
    table, layer, row, per_batch = mod
    if per_batch:
        index_map = lambda *g: (layer, row + batch_of(*g), 0, 0)
    else:
        index_map = lambda *g: (layer, row, 0, 0)
    return pl.BlockSpec((None, None) + table.shape[2:], index_map)


def _rms(x, gain):
    ms = jnp.mean(x * x, axis=-1, keepdims=True)
    return x * lax.rsqrt(ms + RMS_EPS) * gain


def _modulate(x, npre, mod, s):
    return _rms(x, npre[s:s + 1]) * (1.0 + mod[3 * s + 1:3 * s + 2]) + mod[3 * s:3 * s + 1]


def _mod_kernel(c_ref, w_ref, b_ref, o_ref):
    c = c_ref[...]
    s = (c * jax.nn.sigmoid(c)).astype(BF16)
    o_ref[...] = jnp.dot(s, w_ref[...].astype(BF16), preferred_element_type=F32) + b_ref[...]


def _adaln(cond, ada_w, ada_b):
    depth, d, n = ada_w.shape
    rows = cond.shape[0]
    return pl.pallas_call(
        _mod_kernel,
        grid=(depth, n // MOD_COLS),
        in_specs=[
            pl.BlockSpec((rows, d), lambda l, j: (0, 0)),
            pl.BlockSpec((None, d, MOD_COLS), lambda l, j: (l, 0, j)),
            pl.BlockSpec((None, 1, MOD_COLS), lambda l, j: (l, 0, j)),
        ],
        out_specs=pl.BlockSpec((None, rows, MOD_COLS), lambda l, j: (l, 0, j)),
        out_shape=jax.ShapeDtypeStruct((depth, rows, n), F32),
        compiler_params=_cparams(("arbitrary", "arbitrary")),
        name="adaln_mod",
    )(cond, ada_w, ada_b.reshape(depth, 1, n))


def _ffn_kernel(*refs, s, has_mix):
    if has_mix:
        x_ref, a_ref, wo_ref, mod_ref, npre_ref, npost_ref, wg_ref, wu_ref, wd_ref, o_ref = refs
    else:
        x_ref, mod_ref, npre_ref, npost_ref, wg_ref, wu_ref, wd_ref, o_ref = refs
    mod = mod_ref[...]
    npre = npre_ref[...]
    npost = npost_ref[...]
    hid = wg_ref.shape[1]
    sub = min(FFN_SUB_ROWS, x_ref.shape[0])
    blocks = [slice(r, r + sub) for r in range(0, x_ref.shape[0], sub)]

    def prologue(rows):
        x = x_ref[rows, :]
        if has_mix:
            y = jnp.dot(a_ref[rows, :], wo_ref[...], preferred_element_type=F32)
            x = x + mod[5:6] * _rms(y, npost[1:2])
        return x, _modulate(x, npre, mod, s).astype(BF16)

    def swiglu(rows, x, h):
        acts = []
        for c in range(0, hid, FFN_HID_CHUNK):
            g = jnp.dot(h, wg_ref[:, c:c + FFN_HID_CHUNK], preferred_element_type=F32)
            u = jnp.dot(h, wu_ref[:, c:c + FFN_HID_CHUNK], preferred_element_type=F32)
            acts.append((g * jax.nn.sigmoid(g) * u).astype(BF16))
        acc = jnp.dot(jnp.concatenate(acts, axis=1), wd_ref[...], preferred_element_type=F32)
        o_ref[rows, :] = x + FFN_RES_WEIGHT * mod[3 * s + 2:3 * s + 3] * _rms(acc, npost[s:s + 1])

    ready = prologue(blocks[0])
    for k, rows in enumerate(blocks):
        upcoming = prologue(blocks[k + 1]) if k + 1 < len(blocks) else None
        swiglu(rows, *ready)
        ready = upcoming


def _ffn(x2d, rows_per_batch, mod, layer, norm_pre, norm_post, wg, wu, wd, s, mix=None):
    n, d = x2d.shape
    tm = min(FFN_ROWS, n)
    half = (layer, s // 2)
    row_spec = pl.BlockSpec((tm, d), lambda i: (i, 0))
    in_specs = [row_spec]
    args = [x2d]
    if mix is not None:
        a2d, wo, wo_layer = mix
        in_specs += [pl.BlockSpec((tm, a2d.shape[1]), lambda i: (i, 0)), _layer_spec(wo, (wo_layer,))]
        args += [a2d, wo]
    in_specs += [_mod_spec(mod, lambda i: (i * tm) // rows_per_batch),
                 _layer_spec(norm_pre, (layer,)), _layer_spec(norm_post, (layer,)),
                 _layer_spec(wg, half), _layer_spec(wu, half), _layer_spec(wd, half)]
    args += [mod[0], norm_pre, norm_post, wg, wu, wd]
    return pl.pallas_call(
        functools.partial(_ffn_kernel, s=s, has_mix=mix is not None),
        grid=(n // tm,),
        in_specs=in_specs,
        out_specs=row_spec,
        out_shape=jax.ShapeDtypeStruct((n, d), F32),
        compiler_params=_cparams(("arbitrary",)),
        name="mix_ffn" if mix is not None else "ffn",
    )(*args)


def _qkv_kernel(x_ref, mod_ref, npre_ref, cos_ref, sin_ref, w_ref, q_ref, k_ref, v_ref, *, scale):
    d = x_ref.shape[1]
    half = d // NA_HEADS // 2
    reps = d // cos_ref.shape[1]
    mod = mod_ref[...]
    npre = npre_ref[...]
    sub = min(PROJ_SUB_ROWS, x_ref.shape[0])
    blocks = [slice(r, r + sub) for r in range(0, x_ref.shape[0], sub)]
    lane = lax.broadcasted_iota(jnp.int32, (sub, d), 1)
    first = (lane & (2 * half - 1)) < half

    def rope(t, rows):
        cosf = jnp.tile(cos_ref[rows, :], (1, reps))
        sinf = jnp.tile(sin_ref[rows, :], (1, reps))
        partner = jnp.where(first, pltpu.roll(t, d - half, 1), pltpu.roll(t, half, 1))
        return t * cosf + partner * sinf

    hs = [_modulate(x_ref[rows, :], npre, mod, 1).astype(BF16) for rows in blocks]
    for rows, h in zip(blocks, hs):
        q = jnp.dot(h, w_ref[:, 0:d], preferred_element_type=F32)
        q_ref[rows, :] = (rope(q, rows) * scale).astype(BF16)
        k = jnp.dot(h, w_ref[:, d:2 * d], preferred_element_type=F32)
        k_ref[rows, :] = rope(k, rows).astype(BF16)
        v = jnp.dot(h, w_ref[:, 2 * d:3 * d], preferred_element_type=F32)
        v_ref[rows, :] = v.astype(BF16)


def _qkv(x2d, rows_per_batch, mod, layer, norm_pre, cos_t, sin_t, w_qkv, w_layer):
    n, d = x2d.shape
    tm = min(PROJ_ROWS, rows_per_batch)
    tiles_per_batch = rows_per_batch // tm
    row_spec = pl.BlockSpec((tm, d), lambda i: (i, 0))
    tab_spec = pl.BlockSpec((tm, cos_t.shape[1]), lambda i: (i % tiles_per_batch, 0))
    out = jax.ShapeDtypeStruct((n, d), BF16)
    return pl.pallas_call(
        functools.partial(_qkv_kernel, scale=float((d // NA_HEADS) ** -0.5) * LOG2_E),
        grid=(n // tm,),
        in_specs=[row_spec, _mod_spec(mod, lambda i: i // tiles_per_batch),
                  _layer_spec(norm_pre, (layer,)), tab_spec, tab_spec, _layer_spec(w_qkv, (w_layer,))],
        out_specs=[row_spec, row_spec, row_spec],
        out_shape=[out, out, out],
        compiler_params=_cparams(("arbitrary",)),
        name="qkv_rope",
    )(x2d, mod[0], norm_pre, cos_t, sin_t, w_qkv)


_NT = (((1,), (1,)), ((), ()))


def _lane_is_head0(shape):
    return lax.broadcasted_iota(jnp.int32, shape, len(shape) - 1) < shape[-1] // 2


def _head_scores(q, hd, kv_blocks, bias):
    is0 = _lane_is_head0(q.shape)
    qh = jnp.where(is0 if hd == 0 else jnp.logical_not(is0), q, jnp.zeros_like(q))
    scores = []
    for blk, (k, _) in enumerate(kv_blocks):
        s = lax.dot_general(qh, k, _NT, preferred_element_type=F32)
        if blk == 0 and bias is not None:
            s = s + bias
        scores.append(s)
    return scores


def _head_output(scores, hd, kv_blocks):
    m = functools.reduce(jnp.maximum, [jnp.max(s, axis=-1, keepdims=True) for s in scores])
    o = None
    for s, (_, v) in zip(scores, kv_blocks):
        is0 = _lane_is_head0(v.shape)
        v_aug = jnp.where(is0 if hd == 0 else jnp.logical_not(is0), v, jnp.ones_like(v))
        part = jnp.dot(jnp.exp2(s - m).astype(BF16), v_aug, preferred_element_type=F32)
        o = part if o is None else o + part
    return o


def _attend(jobs):
    raw = []
    pending = None
    for q, hd, kv_blocks, bias in jobs:
        scores = _head_scores(q, hd, kv_blocks, bias)
        if pending is not None:
            raw.append(_head_output(*pending))
        pending = (scores, hd, kv_blocks)
    raw.append(_head_output(*pending))
    outs = []
    for o0, o1 in zip(raw[0::2], raw[1::2]):
        is0 = _lane_is_head0(o0.shape)
        num = jnp.where(is0, o0, o1)
        den = pltpu.roll(jnp.where(is0, o1, o0), o0.shape[-1] // 2, 1)
        outs.append(num / den)
    return outs


def _attn_tile_plan(n_rows):
    win_rows = min(NA_WIN_ROWS, n_rows)
    key_rows = ATT_ROWS + NA_WIN_ROWS - 1
    n_blocks = n_rows // ATT_ROWS
    masked = 2 * NA_WIN_ROWS - 1
    plan = np.full((3, ATT_ROWS, key_rows), masked, np.int64)
    for kind, blk in enumerate((0, 1, n_blocks - 1)):
        r_b = blk * ATT_ROWS
        w0 = int(np.clip(r_b - NA_WIN_ROWS // 2, 0, n_rows - key_rows))
        for ri in range(ATT_ROWS):
            r = r_b + ri
            r0 = int(np.clip(r - win_rows // 2, 0, n_rows - win_rows))
            for kri in range(key_rows):
                kr = w0 + kri
                if r0 <= kr < r0 + win_rows:
                    plan[kind, ri, kri] = kr - r + NA_WIN_ROWS - 1
    return plan


def _attn_kernel(q_ref, k_ref, v_ref, kc_ref, vc_ref, band_ref, o_ref, bias_s, *, plan):
    rb = pl.program_id(2)
    n_blocks = pl.num_programs(2)

    def build(kind):
        for h in range(band_ref.shape[0]):
            for ri in range(ATT_ROWS):
                tiles = [band_ref[h, int(e)] for e in plan[kind, ri]]
                bias_s[h // 2, h % 2, ri * GRID_W:(ri + 1) * GRID_W, :] = jnp.concatenate(tiles, axis=1)

    for kind, blk in enumerate((0, 1, n_blocks - 1)):
        pl.when(rb == blk)(functools.partial(build, kind))

    jobs = []
    for pp in range(bias_s.shape[0]):
        lanes = slice(pp * LANES, (pp + 1) * LANES)
        kv_blocks = [(k_ref[0, :, lanes], v_ref[0, :, lanes]), (kc_ref[:, lanes], vc_ref[:, lanes])]
        q = q_ref[:, lanes]
        jobs += [(q, hd, kv_blocks, bias_s[pp, hd]) for hd in range(2)]
    for pp, o in enumerate(_attend(jobs)):
        o_ref[:, pp * LANES:(pp + 1) * LANES] = o.astype(o_ref.dtype)


def _attn_band_tiles(rpb):
    heads, n_dr, n_dc = rpb.shape
    win_cols = min(NA_WIN_COLS, GRID_W)
    lead = GRID_W - NA_WIN_COLS
    padded = jnp.pad(rpb.astype(F32) * LOG2_E, ((0, 0), (0, 0), (lead, 2 * GRID_W - 1 - lead - n_dc)))
    band = jnp.stack([padded[:, :, GRID_W - 1 - qc:2 * GRID_W - 1 - qc] for qc in range(GRID_W)], axis=2)
    col = np.arange(GRID_W)
    c0 = np.clip(col - win_cols // 2, 0, GRID_W - win_cols)
    valid_c = (col[None, :] >= c0[:, None]) & (col[None, :] < c0[:, None] + win_cols)
    band = jnp.where(jnp.asarray(valid_c)[None, None], band, NEG_BIG)
    masked = jnp.full((heads, 1, GRID_W, GRID_W), NEG_BIG, F32)
    return jnp.concatenate([band, masked], axis=1)


def _attention(q, k, v, kc, vc, band):
    bsz, n_tok, d = q.shape
    n_ctx = kc.shape[1]
    n_rows = n_tok // GRID_W
    n_blocks = n_rows // ATT_ROWS
    rq = ATT_ROWS * GRID_W
    nk = (ATT_ROWS + NA_WIN_ROWS - 1) * GRID_W
    pair = ATT_PAIRS * LANES
    q_spec = pl.BlockSpec((None, rq, pair), lambda b, hp, rb: (b, rb, hp))

    def window_map(b, hp, rb):
        w0 = jnp.clip(rb * ATT_ROWS - NA_WIN_ROWS // 2, 0, n_rows - nk // GRID_W)
        return (b, w0 * GRID_W, hp * pair)

    seq_spec = pl.BlockSpec((pl.Element(1), pl.Element(nk), pl.Element(pair)), window_map)
    ctx_spec = pl.BlockSpec((None, n_ctx, pair), lambda b, hp, rb: (b, 0, hp))
    band_spec = pl.BlockSpec((2 * ATT_PAIRS,) + band.shape[1:], lambda b, hp, rb: (hp, 0, 0, 0))
    return pl.pallas_call(
        functools.partial(_attn_kernel, plan=_attn_tile_plan(n_rows)),
        grid=(bsz, d // pair, n_blocks),
        in_specs=[q_spec, seq_spec, seq_spec, ctx_spec, ctx_spec, band_spec],
        out_specs=q_spec,
        out_shape=jax.ShapeDtypeStruct((bsz, n_tok, d), BF16),
        scratch_shapes=[pltpu.VMEM((ATT_PAIRS, 2, rq, nk), F32)],
        compiler_params=_cparams(("arbitrary", "arbitrary", "arbitrary")),
        name="nbr_attention",
    )(q, k, v, kc, vc, band)


def _ctx_attn_kernel(q_ref, k_ref, v_ref, o_ref):
    q = q_ref[...]
    kv_blocks = [(k_ref[...], v_ref[...])]
    (o,) = _attend([(q, hd, kv_blocks, None) for hd in range(2)])
    o_ref[...] = o.astype(o_ref.dtype)


def _ctx_attention(qc, kc, vc):
    bsz, n_ctx, d = qc.shape
    pair = 2 * (d // NA_HEADS)
    spec = pl.BlockSpec((None, n_ctx, pair), lambda b, hp: (b, 0, hp))
    return pl.pallas_call(
        _ctx_attn_kernel,
        grid=(bsz, d // pair),
        in_specs=[spec, spec, spec],
        out_specs=spec,
        out_shape=jax.ShapeDtypeStruct((bsz, n_ctx, d), BF16),
        compiler_params=_cparams(("arbitrary", "arbitrary")),
        name="ctx_attention",
    )(qc, kc, vc)


def _gelu_tanh(x):
    c = float(np.sqrt(2.0 / np.pi))
    return 0.5 * x * (1.0 + jnp.tanh(c * (x + 0.044715 * (x * x * x))))


def _lru_in_kernel(x_ref, xp_ref, xn_ref, mod_ref, npre_ref, wg_ref, wr_ref, cw_ref, cb_ref,
                   u_ref, g_ref, rec_s):
    tm = x_ref.shape[0]
    halo = xp_ref.shape[0]
    i = pl.program_id(1)
    last = pl.num_programs(1) - 1
    mod = mod_ref[...]
    npre = npre_ref[...]
    x_ext = jnp.concatenate([xp_ref[...], x_ref[...], xn_ref[...]], axis=0)
    h_ext = _modulate(x_ext, npre, mod, 1)
    rec = jnp.dot(h_ext.astype(BF16), wr_ref[...], preferred_element_type=F32)
    gate = jnp.dot(h_ext[halo:halo + tm].astype(BF16), wg_ref[...], preferred_element_type=F32)
    row = lax.broadcasted_iota(jnp.int32, (tm + 2 * halo, 1), 0)
    valid = ((row >= halo) | (i > 0)) & ((row < halo + tm) | (i < last))
    rec = jnp.where(valid, rec, 0.0)
    cw = cw_ref[...]
    cb = cb_ref[...]
    n_blk, _, blk_w = u_ref.shape
    seg = tm // SUBLANES
    for n in range(n_blk):
        cols = slice(n * blk_w, (n + 1) * blk_w)
        rec_s[n] = rec[:, cols]
        u = cb[:, cols]
        for tap in range(CONV_W):
            u = u + rec_s[n, pl.ds(halo - CONV_PAD_LEFT + tap, tm), :] * cw[tap:tap + 1, cols]
        for j in range(SUBLANES):
            u_ref[n, pl.ds(j, seg, stride=SUBLANES), :] = u[j * seg:(j + 1) * seg]
    g_ref[...] = _gelu_tanh(gate).astype(g_ref.dtype)


def _lru_in(x3d, mod, layer, norm_pre, w_in, conv_w, conv_b, w_layer):
    bsz, n_tok, d = x3d.shape
    width = w_in.shape[2] // 2
    tm = min(SCAN_ROWS, n_tok)
    n_blk = width // LANES
    halo = SUBLANES
    halo_blocks = n_tok // halo
    per_tile = tm // halo
    out_spec = pl.BlockSpec((None, tm, width), lambda b, i: (b, i, 0))
    u_spec = pl.BlockSpec((None, None, n_blk, tm, LANES), lambda b, i: (b, i, 0, 0, 0))
    return pl.pallas_call(
        _lru_in_kernel,
        grid=(bsz, n_tok // tm),
        in_specs=[
            pl.BlockSpec((None, tm, d), lambda b, i: (b, i, 0)),
            pl.BlockSpec((None, halo, d), lambda b, i: (b, jnp.maximum(i * per_tile - 1, 0), 0)),
            pl.BlockSpec((None, halo, d),
                         lambda b, i: (b, jnp.minimum((i + 1) * per_tile, halo_blocks - 1), 0)),
            _mod_spec(mod, lambda b, i: b),
            _layer_spec(norm_pre, (layer,)),
            _layer_spec(w_in, (w_layer,), cols=(width, 0)), _layer_spec(w_in, (w_layer,), cols=(width, 1)),
            _layer_spec(conv_w, (w_layer,)), _layer_spec(conv_b, (w_layer,)),
        ],
        out_specs=[u_spec, out_spec],
        out_shape=[jax.ShapeDtypeStruct((bsz, n_tok // tm, n_blk, tm, LANES), F32),
                   jax.ShapeDtypeStruct((bsz, n_tok, width), BF16)],
        scratch_shapes=[pltpu.VMEM((n_blk, tm + 2 * halo, LANES), F32)],
        compiler_params=_cparams(("arbitrary", "arbitrary")),
        name="lru_in",
    )(x3d, x3d, x3d, mod[0], norm_pre, w_in, w_in, conv_w, conv_b)


def _scan_kernel(*refs, reverse, combine):
    if combine:
        (u_ref, wcat_ref, ba_ref, bx_ref, lam_ref, init_ref, prev_ref, g_ref,
         o_ref, fin_ref, h_s, p_s, carry_s) = refs
    else:
        (u_ref, wcat_ref, ba_ref, bx_ref, lam_ref, init_ref,
         o_ref, fin_ref, h_s, p_s, carry_s) = refs
    n_blk, tt, blk_w = u_ref.shape
    seg = tt // SUBLANES

    @pl.when(pl.program_id(1) == 0)
    def _():
        carry_s[...] = init_ref[...]

    lam = lam_ref[...]
    softplus = jnp.maximum(-lam, 0.0) + jnp.log1p(jnp.exp(-jnp.abs(lam)))
    half_log2_decay = (-0.5 * LRU_C * LOG2_E) * softplus
    segments = range(SUBLANES - 1, -1, -1) if reverse else range(SUBLANES)

    for n in range(n_blk):
        cols = slice(n * blk_w, (n + 1) * blk_w)
        un = u_ref[n]
        z = jnp.dot(un.astype(BF16), wcat_ref[n], preferred_element_type=F32)
        t_r = jnp.tanh(z[:, :blk_w] + ba_ref[:, cols])
        t_i = jnp.tanh(z[:, blk_w:] + bx_ref[:, cols])
        k = half_log2_decay[:, cols]
        a = jnp.exp2(k + k * t_r)
        one_m = 1.0 - a * a
        p_s[n] = a
        h_s[n] = (one_m * lax.rsqrt(jnp.maximum(one_m, SQRT_TINY))) * ((0.5 + 0.5 * t_i) * un)

    def scan_step(t, carry):
        i = (seg - 1 - t) if reverse else t
        tile = pl.ds(pl.multiple_of(i * SUBLANES, SUBLANES), SUBLANES)
        hs, ps = carry
        new_h, new_p = [], []
        for n in range(n_blk):
            a_t = p_s[n, tile, :]
            h = a_t * hs[n] + h_s[n, tile, :]
            p = a_t * ps[n]
            h_s[n, tile, :] = h
            p_s[n, tile, :] = p
            new_h.append(h)
            new_p.append(p)
        return tuple(new_h), tuple(new_p)

    zeros = tuple(jnp.zeros((SUBLANES, blk_w), F32) for _ in range(n_blk))
    ones = tuple(jnp.ones((SUBLANES, blk_w), F32) for _ in range(n_blk))
    h_end, p_end = lax.fori_loop(0, seg, scan_step, (zeros, ones), unroll=4)

    for n in range(n_blk):
        cols = slice(n * blk_w, (n + 1) * blk_w)
        h, p = h_end[n], p_end[n]
        state = carry_s[:, cols]
        entry = [None] * SUBLANES
        for j in segments:
            entry[j] = state
            state = h[j:j + 1] + p[j:j + 1] * state
        carry_s[:, cols] = state
        for j in range(SUBLANES):
            rows = slice(j * seg, (j + 1) * seg)
            picked = pl.ds(j, seg, stride=SUBLANES)
            ht = h_s[n, picked, :] + p_s[n, picked, :] * entry[j]
            if combine:
                ht = (ht + prev_ref[rows, cols]) * g_ref[rows, cols].astype(F32)
            o_ref[rows, cols] = ht.astype(o_ref.dtype)

    fin_ref[...] = carry_s[...]


def _scan(u, width, wcat, b_a, b_x, lam, init, reverse, prev=None, gate=None):
    bsz, n_chunks, n_blk, tt, blk_w = u.shape
    n_tok = n_chunks * tt
    combine = prev is not None
    if reverse:
        chunk = lambda i: n_chunks - 1 - i
    else:
        chunk = lambda i: i
    u_spec = pl.BlockSpec((None, None, n_blk, tt, blk_w), lambda b, i: (b, chunk(i), 0, 0, 0))
    seq_spec = pl.BlockSpec((None, tt, width), lambda b, i: (b, chunk(i), 0))
    vec_spec = pl.BlockSpec((None, 1, width), lambda b, i: (b, 0, 0))
    in_specs = [u_spec, _const_spec(wcat.shape), _const_spec((1, width)), _const_spec((1, width)),
                _const_spec((1, width)), vec_spec]
    args = [u, wcat, b_a.reshape(1, width), b_x.reshape(1, width), lam.reshape(1, width), init]
    if combine:
        in_specs += [seq_spec, seq_spec]
        args += [prev, gate]
    return pl.pallas_call(
        functools.partial(_scan_kernel, reverse=reverse, combine=combine),
        grid=(bsz, n_chunks),
        in_specs=in_specs,
        out_specs=[seq_spec, vec_spec],
        out_shape=[jax.ShapeDtypeStruct((bsz, n_tok, width), BF16 if combine else F32),
                   jax.ShapeDtypeStruct((bsz, 1, width), F32)],
        scratch_shapes=[pltpu.VMEM((n_blk, tt, blk_w), F32), pltpu.VMEM((n_blk, tt, blk_w), F32),
                        pltpu.VMEM((1, width), F32)],
        compiler_params=_cparams(("arbitrary", "arbitrary")),
        name="lru_scan_bwd" if reverse else "lru_scan_fwd",
    )(*args)


def _bidirectional_lru(u, gate, wcat, b_a, b_x, lam, init_f, init_b):
    width = gate.shape[-1]
    h_f, fin_f = _scan(u, width, wcat[0], b_a[0], b_x[0], lam[0], init_f, reverse=False)
    act, fin_b = _scan(u, width, wcat[1], b_a[1], b_x[1], lam[1], init_b, reverse=True,
                       prev=h_f, gate=gate)
    return act, fin_f, fin_b


def _rope_tables(n_tok, head_dim):
    pairs = head_dim // 4
    t = jnp.arange(n_tok, dtype=jnp.int32)
    row = (t // GRID_W).astype(F32)
    col = (t % GRID_W).astype(F32)
    inv_freq = jnp.power(ROPE_BASE, -jnp.arange(pairs, dtype=F32) / pairs)
    ang = jnp.concatenate([row[:, None] * inv_freq, col[:, None] * inv_freq], axis=-1)
    cos, sin = jnp.cos(ang), jnp.sin(ang)
    cos_h = jnp.concatenate([cos, cos], axis=-1)
    sin_h = jnp.concatenate([-sin, sin], axis=-1)
    return jnp.tile(cos_h, (1, 2)), jnp.tile(sin_h, (1, 2))


def kernel(x, c, ctx, c_ctx, ada_w, ada_b, norm_pre, norm_post, ffn_w_gate, ffn_w_up, ffn_w_down,
           na_w_qkv, na_w_o, na_rpb, lru_w_in, lru_conv_w, lru_conv_b, lru_w_a, lru_b_a, lru_w_x,
           lru_b_x, lru_lambda, lru_w_o):
    bsz, n_tok, d = x.shape
    n_ctx = ctx.shape[1]
    depth = ada_w.shape[0]
    n_mod = ada_w.shape[2] // d
    head_dim = d // NA_HEADS
    lru_width = lru_w_o.shape[1]
    n_mixers = 2

    cond = jnp.concatenate([c, c_ctx[None, :], jnp.zeros((SUBLANES - bsz - 1, d), F32)], axis=0)
    mods = _adaln(cond, ada_w, ada_b).reshape(depth, SUBLANES, n_mod, d)

    cos_t, sin_t = _rope_tables(n_tok, head_dim)
    ones_t = jnp.ones((n_ctx, cos_t.shape[1]), F32)
    zeros_t = jnp.zeros((n_ctx, cos_t.shape[1]), F32)

    wg_all, wu_all, wd_all = (w.astype(BF16) for w in (ffn_w_gate, ffn_w_up, ffn_w_down))
    w_qkv_all, na_wo_all = na_w_qkv.astype(BF16), na_w_o.astype(BF16)
    w_in_all, lru_wo_all = lru_w_in.astype(BF16), lru_w_o.astype(BF16)
    conv_b_all = lru_conv_b.reshape(lru_conv_b.shape[0], 1, lru_width)

    xl = x.reshape(bsz * n_tok, d)
    xc = ctx.reshape(bsz * n_ctx, d)
    for i in range(depth):
        last = i == depth - 1
        j = i // n_mixers
        mod = (mods, i, 0, True)
        mod_c = (mods, i, bsz, False)

        def half_ffn(u2d, rows, m, s, mix=None):
            return _ffn(u2d, rows, m, i, norm_pre, norm_post, wg_all, wu_all, wd_all, s, mix)

        xl = half_ffn(xl, n_tok, mod, 0)
        xc = half_ffn(xc, n_ctx, mod_c, 0)
        if i % n_mixers == 0:
            wo_all = na_wo_all
            q, k, v = _qkv(xl, n_tok, mod, i, norm_pre, cos_t, sin_t, w_qkv_all, j)
            qc, kc, vc = _qkv(xc, n_ctx, mod_c, i, norm_pre, ones_t, zeros_t, w_qkv_all, j)
            shp, shp_c = (bsz, n_tok, d), (bsz, n_ctx, d)
            kc3, vc3 = kc.reshape(shp_c), vc.reshape(shp_c)
            band = _attn_band_tiles(na_rpb[j])
            act = _attention(q.reshape(shp), k.reshape(shp), v.reshape(shp), kc3, vc3, band)
            act = act.reshape(bsz * n_tok, d)
            if not last:
                act_c = _ctx_attention(qc.reshape(shp_c), kc3, vc3).reshape(bsz * n_ctx, d)
        else:
            wo_all = lru_wo_all
            wcat = (0.5 * jnp.concatenate([lru_w_a[j], lru_w_x[j]], axis=-1)).astype(BF16)
            half_b_a, half_b_x = 0.5 * lru_b_a[j], 0.5 * lru_b_x[j]
            u_l, g_l = _lru_in(xl.reshape(bsz, n_tok, d), mod, i, norm_pre, w_in_all,
                               lru_conv_w, conv_b_all, j)
            u_c, g_c = _lru_in(xc.reshape(bsz, n_ctx, d), mod_c, i, norm_pre, w_in_all,
                               lru_conv_w, conv_b_all, j)
            zero_state = jnp.zeros((bsz, 1, lru_width), F32)
            act_c, fin_f, fin_b = _bidirectional_lru(u_c, g_c, wcat, half_b_a, half_b_x,
                                                     lru_lambda[j], zero_state, zero_state)
            act, _, _ = _bidirectional_lru(u_l, g_l, wcat, half_b_a, half_b_x,
                                           lru_lambda[j], fin_f, fin_b)
            act = act.reshape(bsz * n_tok, lru_width)
            act_c = act_c.reshape(bsz * n_ctx, lru_width)
        xl = half_ffn(xl, n_tok, mod, 2, mix=(act, wo_all, j))
        if not last:
            xc = half_ffn(xc, n_ctx, mod_c, 2, mix=(act_c, wo_all, j))
    return xl.reshape(bsz, n_tok, d)
```

```python
import functools

import numpy as np
import jax
import jax.numpy as jnp
from jax import lax
from jax.experimental import pallas as pl
from jax.experimental.pallas import tpu as pltpu

F32 = jnp.float32
BF16 = jnp.bfloat16

GRID_W = 64
NA_HEADS = 16
NA_WIN_ROWS = 8
NA_WIN_COLS = 16
ROPE_BASE = 10000.0
LRU_C = 8.0
CONV_W = 4
CONV_PAD_LEFT = 2
RMS_EPS = 1e-6
FFN_RES_WEIGHT = 0.5

LANES = 128
SUBLANES = 8
MXU_DIM = 256

FFN_ROWS = 1024
FFN_SUB_ROWS = 512
FFN_HID_CHUNK = MXU_DIM
PROJ_ROWS = 1024
PROJ_SUB_ROWS = 512
ATT_ROWS = 4
ATT_PAIRS = 8
SCAN_ROWS = 1024
MOD_COLS = 1024
VMEM_LIMIT = 56 * 1024 * 1024
NEG_BIG = -1e30
SQRT_TINY = 1e-30
LOG2_E = float(np.log2(np.e))


def _cparams(sem):
    return pltpu.CompilerParams(dimension_semantics=sem, vmem_limit_bytes=VMEM_LIMIT)


def _const_spec(shape):
    nd = len(shape)
    return pl.BlockSpec(shape, lambda *_: (0,) * nd, pipeline_mode=pl.Buffered(1))


def _layer_spec(arr, lead, cols=None):
    tail = arr.shape[len(lead):]
    if cols is not None:
        tail = tail[:-1] + (cols[0],)
    idx = tuple(lead) + (0,) * (len(tail) - 1) + (0 if cols is None else cols[1],)
    return pl.BlockSpec((None,) * len(lead) + tail, lambda *_: idx, pipeline_mode=pl.Buffered(1))


def _mod_spec(mod, batch_of):
    table, layer, row, per_batch = mod
    if per_batch:
        index_map = lambda *g: (layer, row + batch_of(*g), 0, 0)
    else:
        index_map = lambda *g: (layer, row, 0, 0)
    return pl.BlockSpec((None, None) + table.shape[2:], index_map)


def _rms(x, gain):
    ms = jnp.mean(x * x, axis=-1, keepdims=True)
    return x * lax.rsqrt(ms + RMS_EPS) * gain


def _modulate(x, npre, mod, s):
    return _rms(x, npre[s:s + 1]) * (1.0 + mod[3 * s + 1:3 * s + 2]) + mod[3 * s:3 * s + 1]


def _mod_kernel(c_ref, w_ref, b_ref, o_ref):
    c = c_ref[...]
    s = (c * jax.nn.sigmoid(c)).astype(BF16)
    o_ref[...] = jnp.dot(s, w_ref[...].astype(BF16), preferred_element_type=F32) + b_ref[...]


def _adaln(cond, ada_w, ada_b):
    depth, d, n = ada_w.shape
    rows = cond.shape[0]
    return pl.pallas_call(
        _mod_kernel,
        grid=(depth, n // MOD_COLS),
        in_specs=[
            pl.BlockSpec((rows, d), lambda l, j: (0, 0)),
            pl.BlockSpec((None, d, MOD_COLS), lambda l, j: (l, 0, j)),
            pl.BlockSpec((None, 1, MOD_COLS), lambda l, j: (l, 0, j)),
        ],
        out_specs=pl.BlockSpec((None, rows, MOD_COLS), lambda l, j: (l, 0, j)),
        out_shape=jax.ShapeDtypeStruct((depth, rows, n), F32),
        compiler_params=_cparams(("arbitrary", "arbitrary")),
        name="adaln_mod",
    )(cond, ada_w, ada_b.reshape(depth, 1, n))


def _ffn_kernel(*refs, s, has_mix, cast_next):
    refs = list(refs)
    x_ref = refs.pop(0)
    a_ref, wo_ref = (refs.pop(0), refs.pop(0)) if has_mix else (None, None)
    mod_ref, npre_ref, npost_ref, wg_ref, wu_ref, wd_ref = refs[:6]
    refs = refs[6:]
    if cast_next:
        ng_src, nu_src, nd_src, o_ref, ng_dst, nu_dst, nd_dst = refs
        ng_dst[...] = ng_src[...].astype(BF16)
        nu_dst[...] = nu_src[...].astype(BF16)

        @pl.when(pl.program_id(0) < cast_next)
        def _():
            nd_dst[...] = nd_src[...].astype(BF16)
    else:
        (o_ref,) = refs
    mod = mod_ref[...]
    npre = npre_ref[...]
    npost = npost_ref[...]
    hid = wg_ref.shape[1]
    sub = min(FFN_SUB_ROWS, x_ref.shape[0])
    blocks = [slice(r, r + sub) for r in range(0, x_ref.shape[0], sub)]

    def prologue(rows):
        x = x_ref[rows, :]
        if has_mix:
            y = jnp.dot(a_ref[rows, :], wo_ref[...], preferred_element_type=F32)
            x = x + mod[5:6] * _rms(y, npost[1:2])
        return x, _modulate(x, npre, mod, s).astype(BF16)

    def swiglu(rows, x, h):
        acts = []
        for c in range(0, hid, FFN_HID_CHUNK):
            g = jnp.dot(h, wg_ref[:, c:c + FFN_HID_CHUNK], preferred_element_type=F32)
            u = jnp.dot(h, wu_ref[:, c:c + FFN_HID_CHUNK], preferred_element_type=F32)
            acts.append((g * jax.nn.sigmoid(g) * u).astype(BF16))
        acc = jnp.dot(jnp.concatenate(acts, axis=1), wd_ref[...], preferred_element_type=F32)
        o_ref[rows, :] = x + FFN_RES_WEIGHT * mod[3 * s + 2:3 * s + 3] * _rms(acc, npost[s:s + 1])

    ready = prologue(blocks[0])
    for k, rows in enumerate(blocks):
        upcoming = prologue(blocks[k + 1]) if k + 1 < len(blocks) else None
        swiglu(rows, *ready)
        ready = upcoming


def _ffn(x2d, rows_per_batch, mod, layer, norm_pre, norm_post, wg, wu, wd, s, mix=None, next_w=None):
    n, d = x2d.shape
    hid = wg.shape[1]
    tm = min(FFN_ROWS, n)
    n_steps = n // tm
    row_spec = pl.BlockSpec((tm, d), lambda i: (i, 0))
    in_specs = [row_spec]
    args = [x2d]
    if mix is not None:
        a2d, wo, wo_layer = mix
        in_specs += [pl.BlockSpec((tm, a2d.shape[1]), lambda i: (i, 0)), _layer_spec(wo, (wo_layer,))]
        args += [a2d, wo]
    in_specs += [_mod_spec(mod, lambda i: (i * tm) // rows_per_batch),
                 _layer_spec(norm_pre, (layer,)), _layer_spec(norm_post, (layer,)),
                 _const_spec(wg.shape), _const_spec(wu.shape), _const_spec(wd.shape)]
    args += [mod[0], norm_pre, norm_post, wg, wu, wd]
    out_specs = [row_spec]
    out_shape = [jax.ShapeDtypeStruct((n, d), F32)]
    cast_steps = 0
    if next_w is not None:
        wg_f32, wu_f32, wd_f32, (nl, nf) = next_w
        up_rows = d // n_steps
        down_rows = LANES
        cast_steps = hid // down_rows
        assert up_rows * n_steps == d and up_rows % (2 * SUBLANES) == 0 and cast_steps <= n_steps
        down_slab = lambda i: jnp.minimum(i, cast_steps - 1)
        in_specs += [pl.BlockSpec((None, None, up_rows, hid), lambda i: (nl, nf, i, 0)),
                     pl.BlockSpec((None, None, up_rows, hid), lambda i: (nl, nf, i, 0)),
                     pl.BlockSpec((None, None, down_rows, d), lambda i: (nl, nf, down_slab(i), 0))]
        args += [wg_f32, wu_f32, wd_f32]
        out_specs += [pl.BlockSpec((up_rows, hid), lambda i: (i, 0)),
                      pl.BlockSpec((up_rows, hid), lambda i: (i, 0)),
                      pl.BlockSpec((down_rows, d), lambda i: (down_slab(i), 0))]
        out_shape += [jax.ShapeDtypeStruct((d, hid), BF16), jax.ShapeDtypeStruct((d, hid), BF16),
                      jax.ShapeDtypeStruct((hid, d), BF16)]
    outs = pl.pallas_call(
        functools.partial(_ffn_kernel, s=s, has_mix=mix is not None, cast_next=cast_steps),
        grid=(n_steps,),
        in_specs=in_specs,
        out_specs=out_specs,
        out_shape=out_shape,
        compiler_params=_cparams(("arbitrary",)),
        name="mix_ffn" if mix is not None else "ffn",
    )(*args)
    return (outs[0], tuple(outs[1:])) if next_w is not None else (outs[0], None)


def _qkv_kernel(x_ref, mod_ref, npre_ref, cos_ref, sin_ref, w_ref, q_ref, k_ref, v_ref, *, scale):
    d = x_ref.shape[1]
    half = d // NA_HEADS // 2
    reps = d // cos_ref.shape[1]
    mod = mod_ref[...]
    npre = npre_ref[...]
    sub = min(PROJ_SUB_ROWS, x_ref.shape[0])
    blocks = [slice(r, r + sub) for r in range(0, x_ref.shape[0], sub)]
    lane = lax.broadcasted_iota(jnp.int32, (sub, d), 1)
    first = (lane & (2 * half - 1)) < half

    def rope(t, rows):
        cosf = jnp.tile(cos_ref[rows, :], (1, reps))
        sinf = jnp.tile(sin_ref[rows, :], (1, reps))
        partner = jnp.where(first, pltpu.roll(t, d - half, 1), pltpu.roll(t, half, 1))
        return t * cosf + partner * sinf

    hs = [_modulate(x_ref[rows, :], npre, mod, 1).astype(BF16) for rows in blocks]
    for rows, h in zip(blocks, hs):
        q = jnp.dot(h, w_ref[:, 0:d], preferred_element_type=F32)
        q_ref[rows, :] = (rope(q, rows) * scale).astype(BF16)
        k = jnp.dot(h, w_ref[:, d:2 * d], preferred_element_type=F32)
        k_ref[rows, :] = rope(k, rows).astype(BF16)
        v = jnp.dot(h, w_ref[:, 2 * d:3 * d], preferred_element_type=F32)
        v_ref[rows, :] = v.astype(BF16)


def _qkv(x2d, rows_per_batch, mod, layer, norm_pre, cos_t, sin_t, w_qkv, w_layer):
    n, d = x2d.shape
    tm = min(PROJ_ROWS, rows_per_batch)
    tiles_per_batch = rows_per_batch // tm
    row_spec = pl.BlockSpec((tm, d), lambda i: (i, 0))
    tab_spec = pl.BlockSpec((tm, cos_t.shape[1]), lambda i: (i % tiles_per_batch, 0))
    out = jax.ShapeDtypeStruct((n, d), BF16)
    return pl.pallas_call(
        functools.partial(_qkv_kernel, scale=float((d // NA_HEADS) ** -0.5) * LOG2_E),
        grid=(n // tm,),
        in_specs=[row_spec, _mod_spec(mod, lambda i: i // tiles_per_batch),
                  _layer_spec(norm_pre, (layer,)), tab_spec, tab_spec, _layer_spec(w_qkv, (w_layer,))],
        out_specs=[row_spec, row_spec, row_spec],
        out_shape=[out, out, out],
        compiler_params=_cparams(("arbitrary",)),
        name="qkv_rope",
    )(x2d, mod[0], norm_pre, cos_t, sin_t, w_qkv)


_NT = (((1,), (1,)), ((), ()))


def _lane_is_head0(shape):
    return lax.broadcasted_iota(jnp.int32, shape, len(shape) - 1) < shape[-1] // 2


def _head_scores(q, hd, kv_blocks, bias):
    is0 = _lane_is_head0(q.shape)
    qh = jnp.where(is0 if hd == 0 else jnp.logical_not(is0), q, jnp.zeros_like(q))
    scores = []
    for blk, (k, _) in enumerate(kv_blocks):
        s = lax.dot_general(qh, k, _NT, preferred_element_type=F32)
        if blk == 0 and bias is not None:
            s = s + bias
        scores.append(s)
    return scores


def _head_output(scores, hd, kv_blocks):
    m = functools.reduce(jnp.maximum, [jnp.max(s, axis=-1, keepdims=True) for s in scores])
    o = None
    for s, (_, v) in zip(scores, kv_blocks):
        is0 = _lane_is_head0(v.shape)
        v_aug = jnp.where(is0 if hd == 0 else jnp.logical_not(is0), v, jnp.ones_like(v))
        part = jnp.dot(jnp.exp2(s - m).astype(BF16), v_aug, preferred_element_type=F32)
        o = part if o is None else o + part
    return o


def _attend(jobs):
    raw = []
    pending = None
    for q, hd, kv_blocks, bias in jobs:
        scores = _head_scores(q, hd, kv_blocks, bias)
        if pending is not None:
            raw.append(_head_output(*pending))
        pending = (scores, hd, kv_blocks)
    raw.append(_head_output(*pending))
    outs = []
    for o0, o1 in zip(raw[0::2], raw[1::2]):
        is0 = _lane_is_head0(o0.shape)
        num = jnp.where(is0, o0, o1)
        den = pltpu.roll(jnp.where(is0, o1, o0), o0.shape[-1] // 2, 1)
        outs.append(num / den)
    return outs


def _attn_tile_plan(n_rows):
    win_rows = min(NA_WIN_ROWS, n_rows)
    key_rows = ATT_ROWS + NA_WIN_ROWS - 1
    n_blocks = n_rows // ATT_ROWS
    masked = 2 * NA_WIN_ROWS - 1
    plan = np.full((3, ATT_ROWS, key_rows), masked, np.int64)
    for kind, blk in enumerate((0, 1, n_blocks - 1)):
        r_b = blk * ATT_ROWS
        w0 = int(np.clip(r_b - NA_WIN_ROWS // 2, 0, n_rows - key_rows))
        for ri in range(ATT_ROWS):
            r = r_b + ri
            r0 = int(np.clip(r - win_rows // 2, 0, n_rows - win_rows))
            for kri in range(key_rows):
                kr = w0 + kri
                if r0 <= kr < r0 + win_rows:
                    plan[kind, ri, kri] = kr - r + NA_WIN_ROWS - 1
    return plan


def _attn_kernel(q_ref, k_ref, v_ref, kc_ref, vc_ref, band_ref, o_ref, bias_s, *, plan):
    rb = pl.program_id(2)
    n_blocks = pl.num_programs(2)

    def build(kind):
        for h in range(band_ref.shape[0]):
            for ri in range(ATT_ROWS):
                tiles = [band_ref[h, int(e)] for e in plan[kind, ri]]
                bias_s[h // 2, h % 2, ri * GRID_W:(ri + 1) * GRID_W, :] = jnp.concatenate(tiles, axis=1)

    for kind, blk in enumerate((0, 1, n_blocks - 1)):
        pl.when(rb == blk)(functools.partial(build, kind))

    jobs = []
    for pp in range(bias_s.shape[0]):
        lanes = slice(pp * LANES, (pp + 1) * LANES)
        kv_blocks = [(k_ref[0, :, lanes], v_ref[0, :, lanes]), (kc_ref[:, lanes], vc_ref[:, lanes])]
        q = q_ref[:, lanes]
        jobs += [(q, hd, kv_blocks, bias_s[pp, hd]) for hd in range(2)]
    for pp, o in enumerate(_attend(jobs)):
        o_ref[:, pp * LANES:(pp + 1) * LANES] = o.astype(o_ref.dtype)


def _attn_band_tiles(rpb):
    heads, n_dr, n_dc = rpb.shape
    win_cols = min(NA_WIN_COLS, GRID_W)
    lead = GRID_W - NA_WIN_COLS
    padded = jnp.pad(rpb.astype(F32) * LOG2_E, ((0, 0), (0, 0), (lead, 2 * GRID_W - 1 - lead - n_dc)))
    band = jnp.stack([padded[:, :, GRID_W - 1 - qc:2 * GRID_W - 1 - qc] for qc in range(GRID_W)], axis=2)
    col = np.arange(GRID_W)
    c0 = np.clip(col - win_cols // 2, 0, GRID_W - win_cols)
    valid_c = (col[None, :] >= c0[:, None]) & (col[None, :] < c0[:, None] + win_cols)
    band = jnp.where(jnp.asarray(valid_c)[None, None], band, NEG_BIG)
    masked = jnp.full((heads, 1, GRID_W, GRID_W), NEG_BIG, F32)
    return jnp.concatenate([band, masked], axis=1)


def _attention(q, k, v, kc, vc, band):
    bsz, n_tok, d = q.shape
    n_ctx = kc.shape[1]
    n_rows = n_tok // GRID_W
    n_blocks = n_rows // ATT_ROWS
    rq = ATT_ROWS * GRID_W
    nk = (ATT_ROWS + NA_WIN_ROWS - 1) * GRID_W
    pair = ATT_PAIRS * LANES
    q_spec = pl.BlockSpec((None, rq, pair), lambda b, hp, rb: (b, rb, hp))

    def window_map(b, hp, rb):
        w0 = jnp.clip(rb * ATT_ROWS - NA_WIN_ROWS // 2, 0, n_rows - nk // GRID_W)
        return (b, w0 * GRID_W, hp * pair)

    seq_spec = pl.BlockSpec((pl.Element(1), pl.Element(nk), pl.Element(pair)), window_map)
    ctx_spec = pl.BlockSpec((None, n_ctx, pair), lambda b, hp, rb: (b, 0, hp))
    band_spec = pl.BlockSpec((2 * ATT_PAIRS,) + band.shape[1:], lambda b, hp, rb: (hp, 0, 0, 0))
    return pl.pallas_call(
        functools.partial(_attn_kernel, plan=_attn_tile_plan(n_rows)),
        grid=(bsz, d // pair, n_blocks),
        in_specs=[q_spec, seq_spec, seq_spec, ctx_spec, ctx_spec, band_spec],
        out_specs=q_spec,
        out_shape=jax.ShapeDtypeStruct((bsz, n_tok, d), BF16),
        scratch_shapes=[pltpu.VMEM((ATT_PAIRS, 2, rq, nk), F32)],
        compiler_params=_cparams(("arbitrary", "arbitrary", "arbitrary")),
        name="nbr_attention",
    )(q, k, v, kc, vc, band)


def _ctx_attn_kernel(q_ref, k_ref, v_ref, o_ref):
    q = q_ref[...]
    kv_blocks = [(k_ref[...], v_ref[...])]
    (o,) = _attend([(q, hd, kv_blocks, None) for hd in range(2)])
    o_ref[...] = o.astype(o_ref.dtype)


def _ctx_attention(qc, kc, vc):
    bsz, n_ctx, d = qc.shape
    pair = 2 * (d // NA_HEADS)
    spec = pl.BlockSpec((None, n_ctx, pair), lambda b, hp: (b, 0, hp))
    return pl.pallas_call(
        _ctx_attn_kernel,
        grid=(bsz, d // pair),
        in_specs=[spec, spec, spec],
        out_specs=spec,
        out_shape=jax.ShapeDtypeStruct((bsz, n_ctx, d), BF16),
        compiler_params=_cparams(("arbitrary", "arbitrary")),
        name="ctx_attention",
    )(qc, kc, vc)


def _gelu_tanh(x):
    c = float(np.sqrt(2.0 / np.pi))
    return 0.5 * x * (1.0 + jnp.tanh(c * (x + 0.044715 * (x * x * x))))


def _lru_in_kernel(x_ref, xp_ref, xn_ref, mod_ref, npre_ref, wg_ref, wr_ref, cw_ref, cb_ref,
                   u_ref, g_ref, rec_s):
    tm = x_ref.shape[0]
    halo = xp_ref.shape[0]
    i = pl.program_id(1)
    last = pl.num_programs(1) - 1
    mod = mod_ref[...]
    npre = npre_ref[...]
    x_ext = jnp.concatenate([xp_ref[...], x_ref[...], xn_ref[...]], axis=0)
    h_ext = _modulate(x_ext, npre, mod, 1)
    rec = jnp.dot(h_ext.astype(BF16), wr_ref[...], preferred_element_type=F32)
    gate = jnp.dot(h_ext[halo:halo + tm].astype(BF16), wg_ref[...], preferred_element_type=F32)
    row = lax.broadcasted_iota(jnp.int32, (tm + 2 * halo, 1), 0)
    valid = ((row >= halo) | (i > 0)) & ((row < halo + tm) | (i < last))
    rec = jnp.where(valid, rec, 0.0)
    cw = cw_ref[...]
    cb = cb_ref[...]
    n_blk, _, blk_w = u_ref.shape
    seg = tm // SUBLANES
    for n in range(n_blk):
        cols = slice(n * blk_w, (n + 1) * blk_w)
        rec_s[n] = rec[:, cols]
        u = cb[:, cols]
        for tap in range(CONV_W):
            u = u + rec_s[n, pl.ds(halo - CONV_PAD_LEFT + tap, tm), :] * cw[tap:tap + 1, cols]
        for j in range(SUBLANES):
            u_ref[n, pl.ds(j, seg, stride=SUBLANES), :] = u[j * seg:(j + 1) * seg]
    g_ref[...] = _gelu_tanh(gate).astype(g_ref.dtype)


def _lru_in(x3d, mod, layer, norm_pre, w_in, conv_w, conv_b, w_layer):
    bsz, n_tok, d = x3d.shape
    width = w_in.shape[2] // 2
    tm = min(SCAN_ROWS, n_tok)
    n_blk = width // LANES
    halo = SUBLANES
    halo_blocks = n_tok // halo
    per_tile = tm // halo
    out_spec = pl.BlockSpec((None, tm, width), lambda b, i: (b, i, 0))
    u_spec = pl.BlockSpec((None, None, n_blk, tm, LANES), lambda b, i: (b, i, 0, 0, 0))
    return pl.pallas_call(
        _lru_in_kernel,
        grid=(bsz, n_tok // tm),
        in_specs=[
            pl.BlockSpec((None, tm, d), lambda b, i: (b, i, 0)),
            pl.BlockSpec((None, halo, d), lambda b, i: (b, jnp.maximum(i * per_tile - 1, 0), 0)),
            pl.BlockSpec((None, halo, d),
                         lambda b, i: (b, jnp.minimum((i + 1) * per_tile, halo_blocks - 1), 0)),
            _mod_spec(mod, lambda b, i: b),
            _layer_spec(norm_pre, (layer,)),
            _layer_spec(w_in, (w_layer,), cols=(width, 0)), _layer_spec(w_in, (w_layer,), cols=(width, 1)),
            _layer_spec(conv_w, (w_layer,)), _layer_spec(conv_b, (w_layer,)),
        ],
        out_specs=[u_spec, out_spec],
        out_shape=[jax.ShapeDtypeStruct((bsz, n_tok // tm, n_blk, tm, LANES), F32),
                   jax.ShapeDtypeStruct((bsz, n_tok, width), BF16)],
        scratch_shapes=[pltpu.VMEM((n_blk, tm + 2 * halo, LANES), F32)],
        compiler_params=_cparams(("arbitrary", "arbitrary")),
        name="lru_in",
    )(x3d, x3d, x3d, mod[0], norm_pre, w_in, w_in, conv_w, conv_b)


def _scan_kernel(*refs, reverse, combine):
    if combine:
        (u_ref, wcat_ref, ba_ref, bx_ref, lam_ref, init_ref, prev_ref, g_ref,
         o_ref, fin_ref, h_s, p_s, carry_s) = refs
    else:
        (u_ref, wcat_ref, ba_ref, bx_ref, lam_ref, init_ref,
         o_ref, fin_ref, h_s, p_s, carry_s) = refs
    n_blk, tt, blk_w = u_ref.shape
    seg = tt // SUBLANES

    @pl.when(pl.program_id(1) == 0)
    def _():
        carry_s[...] = init_ref[...]

    lam = lam_ref[...]
    softplus = jnp.maximum(-lam, 0.0) + jnp.log1p(jnp.exp(-jnp.abs(lam)))
    half_log2_decay = (-0.5 * LRU_C * LOG2_E) * softplus
    segments = range(SUBLANES - 1, -1, -1) if reverse else range(SUBLANES)

    for n in range(n_blk):
        cols = slice(n * blk_w, (n + 1) * blk_w)
        un = u_ref[n]
        z = jnp.dot(un.astype(BF16), wcat_ref[n], preferred_element_type=F32)
        t_r = jnp.tanh(z[:, :blk_w] + ba_ref[:, cols])
        t_i = jnp.tanh(z[:, blk_w:] + bx_ref[:, cols])
        k = half_log2_decay[:, cols]
        a = jnp.exp2(k + k * t_r)
        one_m = 1.0 - a * a
        p_s[n] = a
        h_s[n] = (one_m * lax.rsqrt(jnp.maximum(one_m, SQRT_TINY))) * ((0.5 + 0.5 * t_i) * un)

    def scan_step(t, carry):
        i = (seg - 1 - t) if reverse else t
        tile = pl.ds(pl.multiple_of(i * SUBLANES, SUBLANES), SUBLANES)
        hs, ps = carry
        new_h, new_p = [], []
        for n in range(n_blk):
            a_t = p_s[n, tile, :]
            h = a_t * hs[n] + h_s[n, tile, :]
            p = a_t * ps[n]
            h_s[n, tile, :] = h
            p_s[n, tile, :] = p
            new_h.append(h)
            new_p.append(p)
        return tuple(new_h), tuple(new_p)

    zeros = tuple(jnp.zeros((SUBLANES, blk_w), F32) for _ in range(n_blk))
    ones = tuple(jnp.ones((SUBLANES, blk_w), F32) for _ in range(n_blk))
    h_end, p_end = lax.fori_loop(0, seg, scan_step, (zeros, ones), unroll=4)

    for n in range(n_blk):
        cols = slice(n * blk_w, (n + 1) * blk_w)
        h, p = h_end[n], p_end[n]
        state = carry_s[:, cols]
        entry = [None] * SUBLANES
        for j in segments:
            entry[j] = state
            state = h[j:j + 1] + p[j:j + 1] * state
        carry_s[:, cols] = state
        for j in range(SUBLANES):
            rows = slice(j * seg, (j + 1) * seg)
            picked = pl.ds(j, seg, stride=SUBLANES)
            ht = h_s[n, picked, :] + p_s[n, picked, :] * entry[j]
            if combine:
                ht = (ht + prev_ref[rows, cols]) * g_ref[rows, cols].astype(F32)
            o_ref[rows, cols] = ht.astype(o_ref.dtype)

    fin_ref[...] = carry_s[...]


def _scan(u, width, wcat, b_a, b_x, lam, init, reverse, prev=None, gate=None):
    bsz, n_chunks, n_blk, tt, blk_w = u.shape
    n_tok = n_chunks * tt
    combine = prev is not None
    if reverse:
        chunk = lambda i: n_chunks - 1 - i
    else:
        chunk = lambda i: i
    u_spec = pl.BlockSpec((None, None, n_blk, tt, blk_w), lambda b, i: (b, chunk(i), 0, 0, 0))
    seq_spec = pl.BlockSpec((None, tt, width), lambda b, i: (b, chunk(i), 0))
    vec_spec = pl.BlockSpec((None, 1, width), lambda b, i: (b, 0, 0))
    in_specs = [u_spec, _const_spec(wcat.shape), _const_spec((1, width)), _const_spec((1, width)),
                _const_spec((1, width)), vec_spec]
    args = [u, wcat, b_a.reshape(1, width), b_x.reshape(1, width), lam.reshape(1, width), init]
    if combine:
        in_specs += [seq_spec, seq_spec]
        args += [prev, gate]
    return pl.pallas_call(
        functools.partial(_scan_kernel, reverse=reverse, combine=combine),
        grid=(bsz, n_chunks),
        in_specs=in_specs,
        out_specs=[seq_spec, vec_spec],
        out_shape=[jax.ShapeDtypeStruct((bsz, n_tok, width), BF16 if combine else F32),
                   jax.ShapeDtypeStruct((bsz, 1, width), F32)],
        scratch_shapes=[pltpu.VMEM((n_blk, tt, blk_w), F32), pltpu.VMEM((n_blk, tt, blk_w), F32),
                        pltpu.VMEM((1, width), F32)],
        compiler_params=_cparams(("arbitrary", "arbitrary")),
        name="lru_scan_bwd" if reverse else "lru_scan_fwd",
    )(*args)


def _bidirectional_lru(u, gate, wcat, b_a, b_x, lam, init_f, init_b):
    width = gate.shape[-1]
    h_f, fin_f = _scan(u, width, wcat[0], b_a[0], b_x[0], lam[0], init_f, reverse=False)
    act, fin_b = _scan(u, width, wcat[1], b_a[1], b_x[1], lam[1], init_b, reverse=True,
                       prev=h_f, gate=gate)
    return act, fin_f, fin_b


def _rope_tables(n_tok, head_dim):
    pairs = head_dim // 4
    t = jnp.arange(n_tok, dtype=jnp.int32)
    row = (t // GRID_W).astype(F32)
    col = (t % GRID_W).astype(F32)
    inv_freq = jnp.power(ROPE_BASE, -jnp.arange(pairs, dtype=F32) / pairs)
    ang = jnp.concatenate([row[:, None] * inv_freq, col[:, None] * inv_freq], axis=-1)
    cos, sin = jnp.cos(ang), jnp.sin(ang)
    cos_h = jnp.concatenate([cos, cos], axis=-1)
    sin_h = jnp.concatenate([-sin, sin], axis=-1)
    return jnp.tile(cos_h, (1, 2)), jnp.tile(sin_h, (1, 2))


def kernel(x, c, ctx, c_ctx, ada_w, ada_b, norm_pre, norm_post, ffn_w_gate, ffn_w_up, ffn_w_down,
           na_w_qkv, na_w_o, na_rpb, lru_w_in, lru_conv_w, lru_conv_b, lru_w_a, lru_b_a, lru_w_x,
           lru_b_x, lru_lambda, lru_w_o):
    bsz, n_tok, d = x.shape
    n_ctx = ctx.shape[1]
    depth = ada_w.shape[0]
    n_mod = ada_w.shape[2] // d
    head_dim = d // NA_HEADS
    lru_width = lru_w_o.shape[1]
    n_mixers = 2

    cond = jnp.concatenate([c, c_ctx[None, :], jnp.zeros((SUBLANES - bsz - 1, d), F32)], axis=0)
    mods = _adaln(cond, ada_w, ada_b).reshape(depth, SUBLANES, n_mod, d)

    cos_t, sin_t = _rope_tables(n_tok, head_dim)
    ones_t = jnp.ones((n_ctx, cos_t.shape[1]), F32)
    zeros_t = jnp.zeros((n_ctx, cos_t.shape[1]), F32)

    half_steps = [(i, f) for i in range(depth) for f in range(2)]
    ffn_w = tuple(w[0, 0].astype(BF16) for w in (ffn_w_gate, ffn_w_up, ffn_w_down))
    w_qkv_all, na_wo_all = na_w_qkv.astype(BF16), na_w_o.astype(BF16)
    w_in_all, lru_wo_all = lru_w_in.astype(BF16), lru_w_o.astype(BF16)
    conv_b_all = lru_conv_b.reshape(lru_conv_b.shape[0], 1, lru_width)

    xl = x.reshape(bsz * n_tok, d)
    xc = ctx.reshape(bsz * n_ctx, d)
    for i in range(depth):
        last = i == depth - 1
        j = i // n_mixers
        mod = (mods, i, 0, True)
        mod_c = (mods, i, bsz, False)

        def half_ffn(u2d, rows, m, s, mix=None, cast=False):
            nxt = half_steps.index((i, s // 2)) + 1
            next_w = None
            if cast and nxt < len(half_steps):
                next_w = (ffn_w_gate, ffn_w_up, ffn_w_down, half_steps[nxt])
            return _ffn(u2d, rows, m, i, norm_pre, norm_post, *ffn_w, s, mix, next_w)

        xl, w_mid = half_ffn(xl, n_tok, mod, 0, cast=True)
        xc, _ = half_ffn(xc, n_ctx, mod_c, 0)
        ffn_w = w_mid
        if i % n_mixers == 0:
            wo_all = na_wo_all
            q, k, v = _qkv(xl, n_tok, mod, i, norm_pre, cos_t, sin_t, w_qkv_all, j)
            qc, kc, vc = _qkv(xc, n_ctx, mod_c, i, norm_pre, ones_t, zeros_t, w_qkv_all, j)
            shp, shp_c = (bsz, n_tok, d), (bsz, n_ctx, d)
            kc3, vc3 = kc.reshape(shp_c), vc.reshape(shp_c)
            band = _attn_band_tiles(na_rpb[j])
            act = _attention(q.reshape(shp), k.reshape(shp), v.reshape(shp), kc3, vc3, band)
            act = act.reshape(bsz * n_tok, d)
            if not last:
                act_c = _ctx_attention(qc.reshape(shp_c), kc3, vc3).reshape(bsz * n_ctx, d)
        else:
            wo_all = lru_wo_all
            wcat = (0.5 * jnp.concatenate([lru_w_a[j], lru_w_x[j]], axis=-1)).astype(BF16)
            half_b_a, half_b_x = 0.5 * lru_b_a[j], 0.5 * lru_b_x[j]
            u_l, g_l = _lru_in(xl.reshape(bsz, n_tok, d), mod, i, norm_pre, w_in_all,
                               lru_conv_w, conv_b_all, j)
            u_c, g_c = _lru_in(xc.reshape(bsz, n_ctx, d), mod_c, i, norm_pre, w_in_all,
                               lru_conv_w, conv_b_all, j)
            zero_state = jnp.zeros((bsz, 1, lru_width), F32)
            act_c, fin_f, fin_b = _bidirectional_lru(u_c, g_c, wcat, half_b_a, half_b_x,
                                                     lru_lambda[j], zero_state, zero_state)
            act, _, _ = _bidirectional_lru(u_l, g_l, wcat, half_b_a, half_b_x,
                                           lru_lambda[j], fin_f, fin_b)
            act = act.reshape(bsz * n_tok, lru_width)
            act_c = act_c.reshape(bsz * n_ctx, lru_width)
        xl, w_next = half_ffn(xl, n_tok, mod, 2, mix=(act, wo_all, j), cast=True)
        if not last:
            xc, _ = half_ffn(xc, n_ctx, mod_c, 2, mix=(act_c, wo_all, j))
        ffn_w = w_next
    return xl.reshape(bsz, n_tok, d)
```
